```python
import math
import jax, jax.numpy as jnp
from jax import lax
import numpy as np

D_MODEL = 1024
BATCH = 16
SEQ = 4096
DEPTH = 4
DEC_BATCH = 4
DEC_SEQ = 4096
PAST_LEN = 128

N_EVEN = (DEPTH + 1) // 2
N_ODD = DEPTH // 2
ALPHA = (2.0 * DEPTH) ** 0.25
BETA = (8.0 * DEPTH) ** -0.25
LN_EPS = 1e-5
RMS_EPS = 1e-6

CONV_CH = D_MODEL // 2
CONV_W = 31
MLA_HEADS = 8
MLA_NOPE = 64
MLA_ROPE = 32
MLA_V = 64
MLA_Q_LORA = D_MODEL // 4
MLA_KV_LORA = D_MODEL // 8
ROPE_THETA = 10000.0
Q_BLOCK = 128
EV_SPLITS = (CONV_CH, CONV_CH, MLA_Q_LORA, MLA_KV_LORA, MLA_ROPE)
EV_IN = sum(EV_SPLITS)
EV_MIX = CONV_CH + MLA_HEADS * MLA_V

SSD_HEADS = 8
SSD_HEAD_DIM = 64
SSD_INNER = SSD_HEADS * SSD_HEAD_DIM
SSD_GROUPS = 2
SSD_STATE = 128
SSD_CONV_W = 5
SSD_CHUNK = 128
SSD_XBC = SSD_INNER + 2 * SSD_GROUPS * SSD_STATE
ML_HEADS = 8
ML_HEAD_DIM = 64
ML_INNER = ML_HEADS * ML_HEAD_DIM
ML_CHUNK = 128
OD_SPLITS = (SSD_INNER, SSD_XBC, 2 * SSD_HEADS, ML_INNER, ML_INNER, ML_INNER, ML_INNER, 2 * ML_HEADS, 2 * ML_HEADS)
OD_IN = sum(OD_SPLITS)
OD_MIX = SSD_INNER + ML_INNER

D_FF = 2816
N_EXPERTS = 8
TOP_K = 2
D_FF_EXPERT = 3584

kernel_name = 'hybrid_bidir_encoder_trunk'


def _split(x, sizes):
    return jnp.split(x, np.cumsum(sizes)[:-1].tolist(), axis=-1)


def _standardize(x):
    xf = x.astype(jnp.float32)
    mu = jnp.mean(xf, -1, keepdims=True)
    var = jnp.mean(jnp.square(xf - mu), -1, keepdims=True)
    return ((xf - mu) * lax.rsqrt(var + LN_EPS)).astype(x.dtype)


def _layernorm(x, g, b):
    return _standardize(x) * g + b


def _rmsnorm(x, g):
    xf = x.astype(jnp.float32)
    return (xf * lax.rsqrt(jnp.mean(xf * xf, -1, keepdims=True) + RMS_EPS)).astype(x.dtype) * g


def _dwconv(x, w, b):
    width = w.shape[0]
    y = lax.conv_general_dilated(x, w[:, None, :], (1,), [(width // 2, width // 2)],
                                 dimension_numbers=('NWC', 'WIO', 'NWC'),
                                 feature_group_count=x.shape[-1])
    return y + b


def _rope(x, pos):
    half = x.shape[-1] // 2
    inv_freq = ROPE_THETA ** (-jnp.arange(half, dtype=jnp.float32) / half)
    ang = pos.astype(jnp.float32)[:, None, None] * inv_freq
    cos, sin = jnp.cos(ang).astype(x.dtype), jnp.sin(ang).astype(x.dtype)
    x1, x2 = x[..., :half], x[..., half:]
    return jnp.concatenate([x1 * cos - x2 * sin, x2 * cos + x1 * sin], -1)


def _block_attention(q, k, v):
    B, S, H, Dq = q.shape
    nb = S // Q_BLOCK
    scale = Dq ** -0.5
    qb = q.reshape(B, nb, Q_BLOCK, H, Dq).transpose(1, 0, 2, 3, 4)

    def one(qblk):
        s = jnp.einsum('bqhd,bkhd->bhqk', qblk, k).astype(jnp.float32) * scale
        p = jax.nn.softmax(s, axis=-1).astype(v.dtype)
        return jnp.einsum('bhqk,bkhd->bqhd', p, v)

    o = lax.map(one, qb)
    return o.transpose(1, 0, 2, 3, 4).reshape(B, S, H, v.shape[-1])


def _segsum(a):
    T = a.shape[-1]
    cs = jnp.cumsum(a, axis=-1)
    seg = cs[..., :, None] - cs[..., None, :]
    return jnp.where(jnp.tril(jnp.ones((T, T), dtype=bool)), seg, -jnp.inf)


def _ssd_scan(xh, dt, a, bm, cm):
    B, S, H, P = xh.shape
    G, N = bm.shape[-2], bm.shape[-1]
    R = H // G
    L = SSD_CHUNK
    c = S // L
    f32 = jnp.float32
    dtf = dt.astype(f32)
    x = (xh.astype(f32) * dtf[..., None]).reshape(B, c, L, G, R, P)
    da = (dtf * a.astype(f32)).reshape(B, c, L, G, R).transpose(0, 3, 4, 1, 2)
    bmc = bm.astype(f32).reshape(B, c, L, G, N)
    cmc = cm.astype(f32).reshape(B, c, L, G, N)
    a_cs = jnp.cumsum(da, axis=-1)
    cb = jnp.einsum('bclgn,bcsgn->bcgls', cmc, bmc)
    y_diag = jnp.einsum('bcgls,bgrcls,bcsgrp->bclgrp', cb, jnp.exp(_segsum(da)), x)
    decay_states = jnp.exp(a_cs[..., -1:] - a_cs)
    states = jnp.einsum('bclgn,bgrcl,bclgrp->bcgrpn', bmc, decay_states, x)
    states = jnp.concatenate([jnp.zeros_like(states[:, :1]), states], axis=1)
    chunk_tot = jnp.pad(a_cs[..., -1], ((0, 0), (0, 0), (0, 0), (1, 0)))
    decay_chunk = jnp.exp(_segsum(chunk_tot))
    states = jnp.einsum('bgrzc,bcgrpn->bzgrpn', decay_chunk, states)[:, :-1]
    y_off = jnp.einsum('bclgn,bcgrpn,bgrcl->bclgrp', cmc, states, jnp.exp(a_cs))
    return (y_diag + y_off).reshape(B, S, H, P).astype(xh.dtype)


def _mlstm_chunkwise(q, k, v, i_pre, f_pre):
    B, S, H, Dk = q.shape
    Dv = v.shape[-1]
    L = ML_CHUNK
    c = S // L
    f32 = jnp.float32
    qc = q.astype(f32).reshape(B, c, L, H, Dk)
    kc = k.astype(f32).reshape(B, c, L, H, Dk)
    vc = v.astype(f32).reshape(B, c, L, H, Dv)
    lf = jax.nn.log_sigmoid(f_pre.astype(f32)).reshape(B, c, L, H).transpose(0, 1, 3, 2)
    li = i_pre.astype(f32).reshape(B, c, L, H).transpose(0, 1, 3, 2)
    bcum = jnp.cumsum(lf, axis=-1)
    g = bcum[..., -1]
    causal = jnp.tril(jnp.ones((L, L), dtype=bool))
    dmat = jnp.where(causal, bcum[..., :, None] - bcum[..., None, :] + li[..., None, :], -jnp.inf)
    w_end = g[..., None] - bcum + li
    m_loc = jnp.max(w_end, axis=-1)
    e_end = jnp.exp(w_end - m_loc[..., None])
    s_loc = jnp.einsum('bchl,bclhd,bclhe->bchde', e_end, kc, vc)
    n_loc = jnp.einsum('bchl,bclhd->bchd', e_end, kc)

    def step(carry, inp):
        c_st, n_st, m_st = carry
        s_i, n_i, g_i, ml_i = inp
        m_new = jnp.maximum(g_i + m_st, ml_i)
        a_old = jnp.exp(g_i + m_st - m_new)
        a_new = jnp.exp(ml_i - m_new)
        c_new = a_old[..., None, None] * c_st + a_new[..., None, None] * s_i
        n_new = a_old[..., None] * n_st + a_new[..., None] * n_i
        return (c_new, n_new, m_new), (c_st, n_st, m_st)

    init = (jnp.zeros((B, H, Dk, Dv), f32), jnp.zeros((B, H, Dk), f32), jnp.zeros((B, H), f32))
    seq_in = (jnp.moveaxis(s_loc, 1, 0), jnp.moveaxis(n_loc, 1, 0), jnp.moveaxis(g, 1, 0), jnp.moveaxis(m_loc, 1, 0))
    _, (c_prev, n_prev, m_prev) = lax.scan(step, init, seq_in)
    c_prev = jnp.moveaxis(c_prev, 0, 1)
    n_prev = jnp.moveaxis(n_prev, 0, 1)
    m_prev = jnp.moveaxis(m_prev, 0, 1)
    inter_log = bcum + m_prev[..., None]
    m_t = jnp.maximum(inter_log, jnp.max(dmat, axis=-1))
    w_intra = jnp.exp(dmat - m_t[..., None])
    w_inter = jnp.exp(inter_log - m_t)
    a_mat = jnp.einsum('bclhd,bcshd->bchls', qc, kc) * w_intra
    num = jnp.einsum('bchls,bcshe->bchle', a_mat, vc) + w_inter[..., None] * jnp.einsum('bclhd,bchde->bchle', qc, c_prev)
    den = jnp.sum(a_mat, axis=-1) + w_inter * jnp.einsum('bclhd,bchd->bchl', qc, n_prev)
    h = num / jnp.maximum(jnp.abs(den), jnp.exp(-m_t))[..., None]
    return h.transpose(0, 1, 3, 2, 4).reshape(B, S, H, Dv).astype(q.dtype)


def _even_mixer(x, w_in, dw_w, dw_b, cln_g, cln_b, qn_g, w_uq, kvn_g, w_ukv, w_out):
    B, S, _ = x.shape
    a_val, a_gate, q_lat, kv_lat, k_rot = _split(x @ w_in, EV_SPLITS)
    u = a_val * jax.nn.sigmoid(a_gate)
    u = _dwconv(u, dw_w, dw_b)
    u = jax.nn.silu(_layernorm(u, cln_g, cln_b))
    pos = jnp.arange(S)
    q = (_rmsnorm(q_lat, qn_g) @ w_uq).reshape(B, S, MLA_HEADS, MLA_NOPE + MLA_ROPE)
    kv = (_rmsnorm(kv_lat, kvn_g) @ w_ukv).reshape(B, S, MLA_HEADS, MLA_NOPE + MLA_V)
    k_pe = jnp.broadcast_to(_rope(k_rot[:, :, None, :], pos), (B, S, MLA_HEADS, MLA_ROPE))
    q = jnp.concatenate([q[..., :MLA_NOPE], _rope(q[..., MLA_NOPE:], pos)], -1)
    k = jnp.concatenate([kv[..., :MLA_NOPE], k_pe], -1)
    att = _block_attention(q, k, kv[..., MLA_NOPE:]).reshape(B, S, MLA_HEADS * MLA_V)
    return jnp.concatenate([u, att], -1) @ w_out


def _odd_mixer(x, w_in, cw, cb, dt_bias, a_log, d_skip, ssd_g, ig_b, fg_b, ml_g, w_out):
    B, S, _ = x.shape
    z, xbc, dt_raw, q, k, v, o, ig, fg = _split(x @ w_in, OD_SPLITS)
    xbc = jax.nn.silu(_dwconv(xbc, cw, cb))
    xs, bm, cm = _split(xbc, (SSD_INNER, SSD_GROUPS * SSD_STATE, SSD_GROUPS * SSD_STATE))
    xs = xs.reshape(B, S, SSD_HEADS, SSD_HEAD_DIM)
    bm = bm.reshape(B, S, SSD_GROUPS, SSD_STATE)
    cm = cm.reshape(B, S, SSD_GROUPS, SSD_STATE)
    dt = jax.nn.softplus(dt_raw.reshape(B, S, 2, SSD_HEADS) + dt_bias)
    a = -jnp.exp(a_log)
    y_f = _ssd_scan(xs, dt[:, :, 0], a[0], bm, cm)
    y_b = _ssd_scan(xs[:, ::-1], dt[:, ::-1, 1], a[1], bm[:, ::-1], cm[:, ::-1])[:, ::-1]
    y = (y_f + y_b + xs * d_skip[:, None]).reshape(B, S, SSD_INNER) * jax.nn.silu(z)
    y = _rmsnorm(y.reshape(B, S, SSD_GROUPS, SSD_INNER // SSD_GROUPS),
                 ssd_g.reshape(SSD_GROUPS, SSD_INNER // SSD_GROUPS)).reshape(B, S, SSD_INNER)
    sh = (B, S, ML_HEADS, ML_HEAD_DIM)
    qh, kh, vh = q.reshape(sh), k.reshape(sh) * (ML_HEAD_DIM ** -0.5), v.reshape(sh)
    ig = ig.reshape(B, S, 2, ML_HEADS) + ig_b
    fg = fg.reshape(B, S, 2, ML_HEADS) + fg_b
    h_f = _mlstm_chunkwise(qh, kh, vh, ig[:, :, 0], fg[:, :, 0])
    h_b = _mlstm_chunkwise(qh[:, ::-1], kh[:, ::-1], vh[:, ::-1], ig[:, ::-1, 1], fg[:, ::-1, 1])[:, ::-1]
    h = (_standardize(h_f + h_b) * ml_g.reshape(ML_HEADS, ML_HEAD_DIM)).reshape(B, S, ML_INNER)
    h = jax.nn.sigmoid(o) * h
    return jnp.concatenate([y, h], -1) @ w_out


def _swiglu(x, wg, wu, wd):
    return (jax.nn.silu(x @ wg) * (x @ wu)) @ wd


def _moe(x, router_w, router_b, wg, wu, wd):
    B, S, D = x.shape
    t = x.reshape(B * S, D)
    logits = (t @ router_w).astype(jnp.float32) + router_b
    top_v, top_i = lax.top_k(logits, TOP_K)
    gates = jax.nn.softmax(top_v, axis=-1)
    combine = jnp.sum(jax.nn.one_hot(top_i, N_EXPERTS, dtype=jnp.float32) * gates[..., None], axis=1).astype(x.dtype)
    out = jnp.zeros_like(t)
    for e in range(N_EXPERTS):
        out = out + combine[:, e:e + 1] * _swiglu(t, wg[e], wu[e], wd[e])
    return out.reshape(B, S, D)


def _trunk(x, p):
    for l in range(DEPTH):
        j = l // 2
        if l % 2 == 0:
            m = _even_mixer(x, p['ev_w_in'][j], p['conv_dw_w'][j], p['conv_dw_b'][j], p['conv_ln_g'][j],
                            p['conv_ln_b'][j], p['mla_q_norm_g'][j], p['mla_w_uq'][j], p['mla_kv_norm_g'][j],
                            p['mla_w_ukv'][j], p['ev_w_out'][j])
        else:
            m = _odd_mixer(x, p['od_w_in'][j], p['ssd_conv_w'][j], p['ssd_conv_b'][j], p['ssd_dt_bias'][j],
                           p['ssd_a_log'][j], p['ssd_d'][j], p['ssd_norm_g'][j], p['ml_igate_b'][j],
                           p['ml_fgate_b'][j], p['ml_norm_g'][j], p['od_w_out'][j])
        x = _layernorm(ALPHA * x + m, p['ln1_g'][l], p['ln1_b'][l])
        if l % 2 == 0:
            f = _swiglu(x, p['ffn_w_gate'][j], p['ffn_w_up'][j], p['ffn_w_down'][j])
        else:
            f = _moe(x, p['moe_router_w'][j], p['moe_router_b'][j], p['moe_w_gate'][j], p['moe_w_up'][j],
                     p['moe_w_down'][j])
        x = _layernorm(ALPHA * x + f, p['ln2_g'][l], p['ln2_b'][l])
    return x


def setup_inputs(seed: int = 0) -> dict:
    key = jax.random.key(seed)
    ks = iter(jax.random.split(key, 48))

    def nrm(shape, scale):
        return scale * jax.random.normal(next(ks), shape, jnp.float32)

    def gain(shape):
        return 1.0 + 0.02 * jax.random.normal(next(ks), shape, jnp.float32)

    def unif(shape, lo, hi):
        return jax.random.uniform(next(ks), shape, jnp.float32, lo, hi)

    NE, NO, D = N_EVEN, N_ODD, D_MODEL
    dt0 = jnp.exp(unif((NO, 2, SSD_HEADS), math.log(1e-3), math.log(1e-1)))
    return {
        'x_prompt': nrm((BATCH, SEQ, D), 1.0),
        'x_sample': nrm((DEC_BATCH, DEC_SEQ, D), 1.0),
        'ev_w_in': nrm((NE, D, EV_IN), D ** -0.5),
        'conv_dw_w': nrm((NE, CONV_W, CONV_CH), CONV_W ** -0.5),
        'conv_dw_b': nrm((NE, CONV_CH), 0.02),
        'conv_ln_g': gain((NE, CONV_CH)),
        'conv_ln_b': nrm((NE, CONV_CH), 0.02),
        'mla_q_norm_g': gain((NE, MLA_Q_LORA)),
        'mla_w_uq': nrm((NE, MLA_Q_LORA, MLA_HEADS * (MLA_NOPE + MLA_ROPE)), MLA_Q_LORA ** -0.5),
        'mla_kv_norm_g': gain((NE, MLA_KV_LORA)),
        'mla_w_ukv': nrm((NE, MLA_KV_LORA, MLA_HEADS * (MLA_NOPE + MLA_V)), MLA_KV_LORA ** -0.5),
        'ev_w_out': nrm((NE, EV_MIX, D), BETA * EV_MIX ** -0.5),
        'od_w_in': nrm((NO, D, OD_IN), D ** -0.5),
        'ssd_conv_w': nrm((NO, SSD_CONV_W, SSD_XBC), SSD_CONV_W ** -0.5),
        'ssd_conv_b': nrm((NO, SSD_XBC), 0.02),
        'ssd_dt_bias': dt0 + jnp.log(-jnp.expm1(-dt0)),
        'ssd_a_log': jnp.log(unif((NO, 2, SSD_HEADS), 1.0, 16.0)),
        'ssd_d': gain((NO, SSD_HEADS)),
        'ssd_norm_g': gain((NO, SSD_INNER)),
        'ml_igate_b': nrm((NO, 2, ML_HEADS), 0.1),
        'ml_fgate_b': unif((NO, 2, ML_HEADS), 3.0, 6.0),
        'ml_norm_g': gain((NO, ML_INNER)),
        'od_w_out': nrm((NO, OD_MIX, D), BETA * OD_MIX ** -0.5),
        'ffn_w_gate': nrm((NE, D, D_FF), D ** -0.5),
        'ffn_w_up': nrm((NE, D, D_FF), D ** -0.5),
        'ffn_w_down': nrm((NE, D_FF, D), BETA * D_FF ** -0.5),
        'moe_router_w': nrm((NO, D, N_EXPERTS), D ** -0.5),
        'moe_router_b': nrm((NO, N_EXPERTS), 0.01),
        'moe_w_gate': nrm((NO, N_EXPERTS, D, D_FF_EXPERT), D ** -0.5),
        'moe_w_up': nrm((NO, N_EXPERTS, D, D_FF_EXPERT), D ** -0.5),
        'moe_w_down': nrm((NO, N_EXPERTS, D_FF_EXPERT, D), BETA * D_FF_EXPERT ** -0.5),
        'ln1_g': gain((DEPTH, D)),
        'ln1_b': nrm((DEPTH, D), 0.02),
        'ln2_g': gain((DEPTH, D)),
        'ln2_b': nrm((DEPTH, D), 0.02),
    }


def reference(x_prompt, x_sample, ev_w_in, conv_dw_w, conv_dw_b, conv_ln_g, conv_ln_b, mla_q_norm_g, mla_w_uq,
              mla_kv_norm_g, mla_w_ukv, ev_w_out, od_w_in, ssd_conv_w, ssd_conv_b, ssd_dt_bias, ssd_a_log, ssd_d,
              ssd_norm_g, ml_igate_b, ml_fgate_b, ml_norm_g, od_w_out, ffn_w_gate, ffn_w_up, ffn_w_down,
              moe_router_w, moe_router_b, moe_w_gate, moe_w_up, moe_w_down, ln1_g, ln1_b, ln2_g, ln2_b):
    p = dict(ev_w_in=ev_w_in, conv_dw_w=conv_dw_w, conv_dw_b=conv_dw_b, conv_ln_g=conv_ln_g, conv_ln_b=conv_ln_b,
             mla_q_norm_g=mla_q_norm_g, mla_w_uq=mla_w_uq, mla_kv_norm_g=mla_kv_norm_g, mla_w_ukv=mla_w_ukv,
             ev_w_out=ev_w_out, od_w_in=od_w_in, ssd_conv_w=ssd_conv_w, ssd_conv_b=ssd_conv_b,
             ssd_dt_bias=ssd_dt_bias, ssd_a_log=ssd_a_log, ssd_d=ssd_d, ssd_norm_g=ssd_norm_g,
             ml_igate_b=ml_igate_b, ml_fgate_b=ml_fgate_b, ml_norm_g=ml_norm_g, od_w_out=od_w_out,
             ffn_w_gate=ffn_w_gate, ffn_w_up=ffn_w_up, ffn_w_down=ffn_w_down, moe_router_w=moe_router_w,
             moe_router_b=moe_router_b, moe_w_gate=moe_w_gate, moe_w_up=moe_w_up, moe_w_down=moe_w_down,
             ln1_g=ln1_g, ln1_b=ln1_b, ln2_g=ln2_g, ln2_b=ln2_b)
    y_prompt = _trunk(x_prompt, p)
    y_sample = _trunk(x_sample, p)
    return (y_prompt, y_sample)
```

```python
import functools
import math

import jax
import jax.numpy as jnp
import numpy as np
from jax import lax
from jax.experimental import pallas as pl
from jax.experimental.pallas import tpu as pltpu

F32 = jnp.float32
BF16 = jnp.bfloat16

D_MODEL = 1024
DEPTH = 4
ALPHA = (2.0 * DEPTH) ** 0.25
LN_EPS = 1e-5
RMS_EPS = 1e-6

CONV_CH = 512
CONV_W = 31
MLA_HEADS = 8
MLA_NOPE = 64
MLA_ROPE = 32
MLA_V = 64
MLA_Q_LORA = 256
MLA_KV_LORA = 128
ROPE_THETA = 10000.0
HEAD_PAD = 128
EV_COLS = 2 * CONV_CH + MLA_Q_LORA + MLA_KV_LORA + 2 * HEAD_PAD

SSD_HEADS = 8
SSD_HEAD_DIM = 64
SSD_INNER = 512
SSD_GROUPS = 2
SSD_STATE = 128
SSD_CONV_W = 5
SSD_XBC = 1024
CHUNK = 128
ML_HEADS = 8
ML_HEAD_DIM = 64
ML_INNER = 512
OD_Z, OD_XBC, OD_Q, OD_K, OD_V, OD_O, OD_SMALL = 0, 512, 1536, 2048, 2560, 3072, 3584
OD_COLS = 3712
SM_DT, SM_IG, SM_FG = 0, 16, 32

D_FF = 2816
N_EXPERTS = 8
D_FF_EXPERT = 3584

VMEM_LIMIT = 56 * 1024 * 1024


def _cparams(sem):
    return pltpu.CompilerParams(dimension_semantics=sem, vmem_limit_bytes=VMEM_LIMIT)


def _const_spec(shape):
    nd = len(shape)
    return pl.BlockSpec(shape, lambda *_: (0,) * nd, pipeline_mode=pl.Buffered(1))


def _layernorm(v, g, b):
    mu = jnp.mean(v, -1, keepdims=True)
    d = v - mu
    var = jnp.mean(d * d, -1, keepdims=True)
    return d * lax.rsqrt(var + LN_EPS) * g + b


def _rmsnorm(v, g):
    return v * lax.rsqrt(jnp.mean(v * v, -1, keepdims=True) + RMS_EPS) * g


def _silu(v):
    return v * jax.nn.sigmoid(v)


def _dot(a, b):
    return jnp.dot(a, b, preferred_element_type=F32)


def _dot_nt(a, b):
    return lax.dot_general(a, b, (((1,), (1,)), ((), ())), preferred_element_type=F32)


def _ev_in_kernel(x_ref, cos_ref, sin_ref, w_in_ref, qg_ref, wq_ref, kvg_ref, wkv_ref,
                  u_ref, q_ref, k_ref, v_ref):
    xb = x_ref[...].astype(BF16)
    h = _dot(xb, w_in_ref[...])
    u_ref[...] = h[:, :CONV_CH] * jax.nn.sigmoid(h[:, CONV_CH:2 * CONV_CH])
    c0 = 2 * CONV_CH
    cos = cos_ref[...]
    sin = sin_ref[...]
    ql = _rmsnorm(h[:, c0:c0 + MLA_Q_LORA], qg_ref[...]).astype(BF16)
    qq = _dot(ql, wq_ref[...])
    c1 = c0 + MLA_Q_LORA
    kvl = _rmsnorm(h[:, c1:c1 + MLA_KV_LORA], kvg_ref[...]).astype(BF16)
    kk = _dot(kvl, wkv_ref[...])
    c2 = c1 + MLA_KV_LORA
    kpe = h[:, c2:c2 + HEAD_PAD] * cos + h[:, c2 + HEAD_PAD:c2 + 2 * HEAD_PAD] * sin
    nh = MLA_HEADS * HEAD_PAD
    for hd in range(MLA_HEADS):
        sl = slice(hd * HEAD_PAD, (hd + 1) * HEAD_PAD)
        sl2 = slice(nh + hd * HEAD_PAD, nh + (hd + 1) * HEAD_PAD)
        q_ref[:, sl] = (qq[:, sl] * cos + qq[:, sl2] * sin).astype(BF16)
        k_ref[:, sl] = (kk[:, sl] + kpe).astype(BF16)
    v_ref[...] = kk[:, nh:].astype(BF16)


def _ev_in(x2, cos_t, sin_t, w_in, qg, wq, kvg, wkv, seq, tm=512):
    T = x2.shape[0]
    nps = seq // tm
    row = lambda i: (i, 0)
    pos = lambda i: (i % nps, 0)
    nh = MLA_HEADS * HEAD_PAD
    return pl.pallas_call(
        _ev_in_kernel,
        grid=(T // tm,),
        in_specs=[
            pl.BlockSpec((tm, D_MODEL), row),
            pl.BlockSpec((tm, HEAD_PAD), pos),
            pl.BlockSpec((tm, HEAD_PAD), pos),
            _const_spec(w_in.shape), _const_spec(qg.shape), _const_spec(wq.shape),
            _const_spec(kvg.shape), _const_spec(wkv.shape),
        ],
        out_specs=[
            pl.BlockSpec((tm, CONV_CH), row),
            pl.BlockSpec((tm, nh), row),
            pl.BlockSpec((tm, nh), row),
            pl.BlockSpec((tm, nh), row),
        ],
        out_shape=[
            jax.ShapeDtypeStruct((T, CONV_CH), F32),
            jax.ShapeDtypeStruct((T, nh), BF16),
            jax.ShapeDtypeStruct((T, nh), BF16),
            jax.ShapeDtypeStruct((T, nh), BF16),
        ],
        compiler_params=_cparams(("parallel",)),
        name="ev_in",
    )(x2, cos_t, sin_t, w_in, qg, wq, kvg, wkv)


def _dwconv_kernel(x_ref, w_ref, b_ref, g_ref, beta_ref, o_ref, pad_ref, tmp_ref, *, width, halo, rows, with_ln):
    S, C = x_ref.shape
    half = width // 2
    win = rows + 2 * halo
    pad_ref[0:halo, :] = jnp.zeros((halo, C), F32)
    pad_ref[halo + S:halo + S + halo, :] = jnp.zeros((halo, C), F32)
    pad_ref[halo:halo + S, :] = x_ref[...]

    def tile(t, carry):
        r0 = pl.multiple_of(t * rows, rows)
        for cb in range(C // 128):
            cs = slice(cb * 128, (cb + 1) * 128)
            window = pad_ref[pl.ds(r0, win), cs]
            acc = jnp.zeros((rows, 128), F32) + b_ref[:, cs]
            for w in range(width):
                off = halo + w - half
                shifted = window if off == 0 else pltpu.roll(window, win - off, 0)
                acc = acc + shifted[:rows] * w_ref[w:w + 1, cs]
            tmp_ref[:, cs] = acc
        acc = tmp_ref[...]
        if with_ln:
            acc = _layernorm(acc, g_ref[...], beta_ref[...])
        o_ref[pl.ds(r0, rows), :] = _silu(acc).astype(o_ref.dtype)
        return carry

    lax.fori_loop(0, S // rows, tile, 0)


def _dwconv(x3, col_block, w, b, g, beta, *, with_ln, out_dtype, ncb=1, rows=128):
    B, S, _ = x3.shape
    C = 512
    width = w.shape[0]
    halo = 16
    assert width // 2 <= halo
    kern = functools.partial(_dwconv_kernel, width=width, halo=halo, rows=rows, with_ln=with_ln)
    return pl.pallas_call(
        kern,
        grid=(B, ncb),
        in_specs=[
            pl.BlockSpec((None, S, C), lambda b, c: (b, 0, col_block + c)),
            pl.BlockSpec((width, C), lambda b, c: (0, c)),
            pl.BlockSpec((1, C), lambda b, c: (0, c)),
            pl.BlockSpec((1, C), lambda b, c: (0, c)),
            pl.BlockSpec((1, C), lambda b, c: (0, c)),
        ],
        out_specs=pl.BlockSpec((None, S, C), lambda b, c: (b, 0, c)),
        out_shape=jax.ShapeDtypeStruct((B, S, C * ncb), out_dtype),
        scratch_shapes=[pltpu.VMEM((S + 2 * halo, C), F32), pltpu.VMEM((rows, C), F32)],
        compiler_params=_cparams(("parallel", "parallel")),
        name="dwconv_ln" if with_ln else "dwconv",
    )(x3, w, b, g, beta)


def _attn_kernel(q_ref, k_ref, v_ref, o_ref, *, scale):
    acc = None
    for j in range(2):
        sl = slice(j * HEAD_PAD, (j + 1) * HEAD_PAD)
        s = _dot_nt(q_ref[:, sl], k_ref[:, sl]) * scale
        m = jnp.max(s, -1, keepdims=True)
        p = jnp.exp(s - m)
        l = jnp.sum(p, -1, keepdims=True)
        o = _dot(p.astype(BF16), v_ref[:, sl]) / l
        acc = o if acc is None else acc + o
    o_ref[...] = acc.astype(o_ref.dtype)


def _attention(q, k, v, B, S, tq=256):
    nq = S // tq
    T = B * S
    k3 = k.reshape(B, S, -1)
    v3 = v.reshape(B, S, -1)
    scale = (MLA_NOPE + MLA_ROPE) ** -0.5
    return pl.pallas_call(
        functools.partial(_attn_kernel, scale=scale),
        grid=(B, MLA_HEADS // 2, nq),
        in_specs=[
            pl.BlockSpec((tq, 2 * HEAD_PAD), lambda b, hp, i: (b * nq + i, hp)),
            pl.BlockSpec((None, S, 2 * HEAD_PAD), lambda b, hp, i: (b, 0, hp)),
            pl.BlockSpec((None, S, 2 * HEAD_PAD), lambda b, hp, i: (b, 0, hp)),
        ],
        out_specs=pl.BlockSpec((tq, 2 * MLA_V), lambda b, hp, i: (b * nq + i, hp)),
        out_shape=jax.ShapeDtypeStruct((T, MLA_HEADS * MLA_V), BF16),
        compiler_params=_cparams(("parallel", "parallel", "parallel")),
        name="attention",
    )(q, k3, v3)


def _ev_out_kernel(x_ref, u_ref, a_ref, w1_ref, w2_ref, g_ref, b_ref, o_ref):
    m = _dot(u_ref[...], w1_ref[...]) + _dot(a_ref[...], w2_ref[...])
    o_ref[...] = _layernorm(ALPHA * x_ref[...] + m, g_ref[...], b_ref[...])


def _ev_out(x2, u, att, w1, w2, g, b, tm=512):
    T = x2.shape[0]
    row = lambda i: (i, 0)
    return pl.pallas_call(
        _ev_out_kernel,
        grid=(T // tm,),
        in_specs=[
            pl.BlockSpec((tm, D_MODEL), row),
            pl.BlockSpec((tm, CONV_CH), row),
            pl.BlockSpec((tm, MLA_HEADS * MLA_V), row),
            _const_spec(w1.shape), _const_spec(w2.shape), _const_spec(g.shape), _const_spec(b.shape),
        ],
        out_specs=pl.BlockSpec((tm, D_MODEL), row),
        out_shape=jax.ShapeDtypeStruct((T, D_MODEL), F32),
        compiler_params=_cparams(("parallel",)),
        name="ev_out",
    )(x2, u, att, w1, w2, g, b)


def _ffn_kernel(x_ref, wg_ref, wu_ref, wd_ref, g_ref, b_ref, o_ref, *, fc):
    x = x_ref[...]
    xb = x.astype(BF16)
    acc = jnp.zeros(x.shape, F32)
    for c in range(wg_ref.shape[1] // fc):
        sl = slice(c * fc, (c + 1) * fc)
        hh = _silu(_dot(xb, wg_ref[:, sl])) * _dot(xb, wu_ref[:, sl])
        acc = acc + _dot(hh.astype(BF16), wd_ref[sl, :])
    o_ref[...] = _layernorm(ALPHA * x + acc, g_ref[...], b_ref[...])


def _ffn(x2, wg, wu, wd, g, b, tm=512, fc=256):
    T = x2.shape[0]
    row = lambda i: (i, 0)
    return pl.pallas_call(
        functools.partial(_ffn_kernel, fc=fc),
        grid=(T // tm,),
        in_specs=[
            pl.BlockSpec((tm, D_MODEL), row),
            _const_spec(wg.shape), _const_spec(wu.shape), _const_spec(wd.shape),
            _const_spec(g.shape), _const_spec(b.shape),
        ],
        out_specs=pl.BlockSpec((tm, D_MODEL), row),
        out_shape=jax.ShapeDtypeStruct((T, D_MODEL), F32),
        compiler_params=_cparams(("parallel",)),
        name="ffn",
    )(x2, wg, wu, wd, g, b)


def _od_in_kernel(x_ref, w_ref, o_ref):
    o_ref[...] = _dot(x_ref[...].astype(BF16), w_ref[...])


def _od_in(x2, w, tm=512):
    T = x2.shape[0]
    row = lambda i: (i, 0)
    return pl.pallas_call(
        _od_in_kernel,
        grid=(T // tm,),
        in_specs=[pl.BlockSpec((tm, D_MODEL), row), _const_spec(w.shape)],
        out_specs=pl.BlockSpec((tm, OD_COLS), row),
        out_shape=jax.ShapeDtypeStruct((T, OD_COLS), F32),
        compiler_params=_cparams(("parallel",)),
        name="od_in",
    )(x2, w)


def _tri(reverse):
    i = lax.broadcasted_iota(jnp.int32, (CHUNK, CHUNK), 0)
    j = lax.broadcasted_iota(jnp.int32, (CHUNK, CHUNK), 1)
    mask = (j >= i) if reverse else (j <= i)
    return mask, mask.astype(F32)


def _softplus(v):
    return jnp.maximum(v, 0.0) + jnp.log1p(jnp.exp(-jnp.abs(v)))


def _ssd_kernel(xbc_ref, sm_ref, bias_ref, a_ref, y_ref, st_ref, *, reverse, lane0):
    c = pl.program_id(1)

    @pl.when(c == 0)
    def _():
        st_ref[...] = jnp.zeros(st_ref.shape, F32)

    mask, tri = _tri(reverse)
    last = 0 if reverse else CHUNK - 1
    dt_all = _softplus(sm_ref[...] + bias_ref[...])
    da_all = dt_all * a_ref[...]
    cs_all = jnp.dot(tri, da_all, preferred_element_type=F32, precision=lax.Precision.HIGHEST)
    cs_t = cs_all.T
    tot_all = cs_all[last:last + 1, :]
    P = SSD_HEAD_DIM
    for g in range(SSD_GROUPS):
        b0 = SSD_INNER + g * SSD_STATE
        c0 = SSD_INNER + SSD_GROUPS * SSD_STATE + g * SSD_STATE
        bm = xbc_ref[:, b0:b0 + SSD_STATE]
        cm = xbc_ref[:, c0:c0 + SSD_STATE].astype(BF16)
        bm_t = bm.T.astype(BF16)
        cb = _dot_nt(cm, bm.astype(BF16))
        for r in range(SSD_HEADS // SSD_GROUPS):
            hd = g * (SSD_HEADS // SSD_GROUPS) + r
            ln = lane0 + hd
            cs_col = cs_all[:, ln:ln + 1]
            cs_row = cs_t[ln:ln + 1, :]
            tot = tot_all[:, ln:ln + 1]
            decay = jnp.exp(jnp.where(mask, cs_col - cs_row, -jnp.inf))
            xh = xbc_ref[:, hd * P:(hd + 1) * P] * dt_all[:, ln:ln + 1]
            y_diag = _dot((cb * decay).astype(BF16), xh.astype(BF16))
            st = st_ref[hd]
            y_off = _dot(cm, st.astype(BF16)) * jnp.exp(cs_col)
            y_ref[:, hd * P:(hd + 1) * P] = y_diag + y_off
            xw = (xh * jnp.exp(tot - cs_col)).astype(BF16)
            st_ref[hd] = jnp.exp(tot) * st + _dot(bm_t, xw)


def _ssd(xbc3, proj3, bias_row, a_row, *, reverse, direction):
    B, S, _ = xbc3.shape
    nc = S // CHUNK
    cidx = (lambda c: nc - 1 - c) if reverse else (lambda c: c)
    kern = functools.partial(_ssd_kernel, reverse=reverse, lane0=SM_DT + direction * SSD_HEADS)
    return pl.pallas_call(
        kern,
        grid=(B, nc),
        in_specs=[
            pl.BlockSpec((None, CHUNK, SSD_XBC), lambda b, c: (b, cidx(c), 0)),
            pl.BlockSpec((None, CHUNK, 128), lambda b, c: (b, cidx(c), OD_SMALL // 128)),
            _const_spec(bias_row.shape), _const_spec(a_row.shape),
        ],
        out_specs=pl.BlockSpec((None, CHUNK, SSD_INNER), lambda b, c: (b, cidx(c), 0)),
        out_shape=jax.ShapeDtypeStruct((B, S, SSD_INNER), F32),
        scratch_shapes=[pltpu.VMEM((SSD_HEADS, SSD_STATE, SSD_HEAD_DIM), F32)],
        compiler_params=_cparams(("parallel", "arbitrary")),
        name="ssd_rev" if reverse else "ssd_fwd",
    )(xbc3, proj3, bias_row, a_row)


def _mlstm_kernel(q_ref, k_ref, v_ref, sm_ref, bias_ref, h_ref, c_ref, n_ref, m_ref, *, reverse, direction):
    c = pl.program_id(1)

    @pl.when(c == 0)
    def _():
        c_ref[...] = jnp.zeros(c_ref.shape, F32)
        n_ref[...] = jnp.zeros(n_ref.shape, F32)
        m_ref[...] = jnp.zeros(m_ref.shape, F32)

    mask, tri = _tri(reverse)
    last = 0 if reverse else CHUNK - 1
    pre = sm_ref[...] + bias_ref[...]
    lf_all = jnp.minimum(pre, 0.0) - jnp.log1p(jnp.exp(-jnp.abs(pre)))
    bc_all = jnp.dot(tri, lf_all, preferred_element_type=F32, precision=lax.Precision.HIGHEST)
    bc_t = bc_all.T
    li_t = pre.T
    g_all = bc_all[last:last + 1, :]
    Dh = ML_HEAD_DIM
    for hd in range(ML_HEADS):
        fl = SM_FG + direction * ML_HEADS + hd
        il = SM_IG + direction * ML_HEADS + hd
        b_col = bc_all[:, fl:fl + 1]
        b_row = bc_t[fl:fl + 1, :]
        li_col = pre[:, il:il + 1]
        li_row = li_t[il:il + 1, :]
        g = g_all[:, fl:fl + 1]
        hs = slice(hd * Dh, (hd + 1) * Dh)
        qh = q_ref[:, hs].astype(BF16)
        kf = k_ref[:, hs] * (Dh ** -0.5)
        kh = kf.astype(BF16)
        vh = v_ref[:, hs].astype(BF16)
        c_prev = c_ref[hd]
        n_prev = n_ref[hd]
        m_prev = m_ref[hd][0:1, 0:1]
        dm = jnp.where(mask, b_col - b_row + li_row, -jnp.inf)
        inter = b_col + m_prev
        m_t = jnp.maximum(inter, jnp.max(dm, -1, keepdims=True))
        w_intra = jnp.exp(dm - m_t)
        w_inter = jnp.exp(inter - m_t)
        a_mat = _dot_nt(qh, kh) * w_intra
        num = _dot(a_mat.astype(BF16), vh) + w_inter * _dot(qh, c_prev.astype(BF16))
        qn = jnp.sum(q_ref[:, hs] * n_prev, -1, keepdims=True)
        den = jnp.sum(a_mat, -1, keepdims=True) + w_inter * qn
        h_ref[:, hs] = num / jnp.maximum(jnp.abs(den), jnp.exp(-m_t))
        w_end = g - b_col + li_col
        m_loc = jnp.max(w_end, 0, keepdims=True)
        ke = kf * jnp.exp(w_end - m_loc)
        s_loc = _dot(ke.T.astype(BF16), vh)
        n_loc = jnp.sum(ke, 0, keepdims=True)
        m_new = jnp.maximum(g + m_prev, m_loc)
        a_old = jnp.exp(g + m_prev - m_new)
        a_new = jnp.exp(m_loc - m_new)
        c_ref[hd] = a_old * c_prev + a_new * s_loc
        n_ref[hd] = a_old * n_prev + a_new * n_loc
        m_ref[hd] = jnp.zeros((8, 128), F32) + m_new


def _mlstm(proj3, bias_row, *, reverse, direction):
    B, S, _ = proj3.shape
    nc = S // CHUNK
    cidx = (lambda c: nc - 1 - c) if reverse else (lambda c: c)
    kern = functools.partial(_mlstm_kernel, reverse=reverse, direction=direction)
    col = lambda cb: (lambda b, c: (b, cidx(c), cb))
    return pl.pallas_call(
        kern,
        grid=(B, nc),
        in_specs=[
            pl.BlockSpec((None, CHUNK, ML_INNER), col(OD_Q // 512)),
            pl.BlockSpec((None, CHUNK, ML_INNER), col(OD_K // 512)),
            pl.BlockSpec((None, CHUNK, ML_INNER), col(OD_V // 512)),
            pl.BlockSpec((None, CHUNK, 128), col(OD_SMALL // 128)),
            _const_spec(bias_row.shape),
        ],
        out_specs=pl.BlockSpec((None, CHUNK, ML_INNER), col(0)),
        out_shape=jax.ShapeDtypeStruct((B, S, ML_INNER), F32),
        scratch_shapes=[
            pltpu.VMEM((ML_HEADS, ML_HEAD_DIM, ML_HEAD_DIM), F32),
            pltpu.VMEM((ML_HEADS, 1, ML_HEAD_DIM), F32),
            pltpu.VMEM((ML_HEADS, 8, 128), F32),
        ],
        compiler_params=_cparams(("parallel", "arbitrary")),
        name="mlstm_rev" if reverse else "mlstm_fwd",
    )(proj3, proj3, proj3, proj3, bias_row)


def _od_out_kernel(x_ref, z_ref, xs_ref, o_ref, yf_ref, yb_ref, hf_ref, hb_ref, dsk_ref, sg_ref, mg_ref,
                   w1_ref, w2_ref, g_ref, b_ref, rw_ref, rb_ref, x1_ref, route_ref):
    y = (yf_ref[...] + yb_ref[...] + xs_ref[...] * dsk_ref[...]) * _silu(z_ref[...])
    gw = SSD_INNER // SSD_GROUPS
    m = None
    for g in range(SSD_GROUPS):
        sl = slice(g * gw, (g + 1) * gw)
        yn = _rmsnorm(y[:, sl], sg_ref[:, sl]).astype(BF16)
        t = _dot(yn, w1_ref[sl, :])
        m = t if m is None else m + t
    hs = hf_ref[...] + hb_ref[...]
    gate = jax.nn.sigmoid(o_ref[...])
    for hd in range(ML_HEADS):
        sl = slice(hd * ML_HEAD_DIM, (hd + 1) * ML_HEAD_DIM)
        v = hs[:, sl]
        mu = jnp.mean(v, -1, keepdims=True)
        dv = v - mu
        var = jnp.mean(dv * dv, -1, keepdims=True)
        hn = (gate[:, sl] * (dv * lax.rsqrt(var + LN_EPS) * mg_ref[:, sl])).astype(BF16)
        m = m + _dot(hn, w2_ref[sl, :])
    x1 = _layernorm(ALPHA * x_ref[...] + m, g_ref[...], b_ref[...])
    x1_ref[...] = x1
    logits = _dot(x1.astype(BF16), rw_ref[...]) + rb_ref[...]
    lane = lax.broadcasted_iota(jnp.int32, logits.shape, 1).astype(F32)
    m1 = jnp.max(logits, -1, keepdims=True)
    i1 = jnp.min(jnp.where(logits == m1, lane, 128.0), -1, keepdims=True)
    rest = jnp.where(lane == i1, -jnp.inf, logits)
    m2 = jnp.max(rest, -1, keepdims=True)
    i2 = jnp.min(jnp.where(rest == m2, lane, 128.0), -1, keepdims=True)
    e = jnp.exp(m2 - m1)
    g1 = 1.0 / (1.0 + e)
    g2 = e / (1.0 + e)
    route_ref[...] = jnp.where(lane == 0.0, i1,
                               jnp.where(lane == 1.0, i2, jnp.where(lane == 2.0, g1, jnp.where(lane == 3.0, g2, 0.0))))


def _od_out(x2, proj, xbc, yf, yb, hf, hb, dsk, sg, mg, w1, w2, g, b, rw, rb, tm=256):
    T = x2.shape[0]
    row = lambda i: (i, 0)
    colb = lambda cb: (lambda i: (i, cb))
    return pl.pallas_call(
        _od_out_kernel,
        grid=(T // tm,),
        in_specs=[
            pl.BlockSpec((tm, D_MODEL), row),
            pl.BlockSpec((tm, 512), colb(OD_Z // 512)),
            pl.BlockSpec((tm, 512), colb(0)),
            pl.BlockSpec((tm, 512), colb(OD_O // 512)),
            pl.BlockSpec((tm, 512), row), pl.BlockSpec((tm, 512), row),
            pl.BlockSpec((tm, 512), row), pl.BlockSpec((tm, 512), row),
            _const_spec(dsk.shape), _const_spec(sg.shape), _const_spec(mg.shape),
            _const_spec(w1.shape), _const_spec(w2.shape), _const_spec(g.shape), _const_spec(b.shape),
            _const_spec(rw.shape), _const_spec(rb.shape),
        ],
        out_specs=[pl.BlockSpec((tm, D_MODEL), row), pl.BlockSpec((tm, 128), row)],
        out_shape=[jax.ShapeDtypeStruct((T, D_MODEL), F32), jax.ShapeDtypeStruct((T, 128), F32)],
        compiler_params=_cparams(("parallel",)),
        name="od_out",
    )(x2, proj, xbc, proj, yf, yb, hf, hb, dsk, sg, mg, w1, w2, g, b, rw, rb)


def _moe_kernel(te_ref, tv_ref, tok_ref, gate_ref, x_hbm, wg_ref, wu_ref, wd_ref, o_ref,
                xbuf, xb_ref, acc_ref, sem, *, tm, nfc):
    i = pl.program_id(0)
    j = pl.program_id(1)
    valid = tv_ref[i] == 1

    def row_copy(r):
        return pltpu.make_async_copy(x_hbm.at[pl.ds(tok_ref[0, r], 1)], xbuf.at[pl.ds(r, 1)], sem)

    @pl.when(jnp.logical_and(valid, j == 0))
    def _():
        def start(r, carry):
            row_copy(r).start()
            return carry

        lax.fori_loop(0, tm, start, 0, unroll=8)

        def wait(r, carry):
            row_copy(r).wait()
            return carry

        lax.fori_loop(0, tm, wait, 0, unroll=8)
        xb_ref[...] = xbuf[...].astype(BF16)
        acc_ref[...] = jnp.zeros(acc_ref.shape, F32)

    @pl.when(valid)
    def _():
        xb = xb_ref[...]
        hh = _silu(_dot(xb, wg_ref[...])) * _dot(xb, wu_ref[...])
        acc_ref[...] += _dot(hh.astype(BF16), wd_ref[...])

    @pl.when(j == nfc - 1)
    def _():
        o_ref[...] = jnp.where(valid, acc_ref[...] * gate_ref[...], 0.0)


def _moe(x1, tile_expert, tile_valid, row_token, row_gate, wg, wu, wd, tm, fc=512):
    n_tiles = tile_expert.shape[0]
    nfc = D_FF_EXPERT // fc
    fcol = lambda i, j, te, tv: (te[i], 0, jnp.where(tv[i] == 1, j, nfc - 1))
    frow = lambda i, j, te, tv: (te[i], jnp.where(tv[i] == 1, j, nfc - 1), 0)
    grid_spec = pltpu.PrefetchScalarGridSpec(
        num_scalar_prefetch=2,
        grid=(n_tiles, nfc),
        in_specs=[
            pl.BlockSpec((None, 1, tm), lambda i, j, te, tv: (i, 0, 0), memory_space=pltpu.SMEM),
            pl.BlockSpec((tm, 1), lambda i, j, te, tv: (i, 0)),
            pl.BlockSpec(memory_space=pl.ANY),
            pl.BlockSpec((None, D_MODEL, fc), fcol),
            pl.BlockSpec((None, D_MODEL, fc), fcol),
            pl.BlockSpec((None, fc, D_MODEL), frow),
        ],
        out_specs=pl.BlockSpec((tm, D_MODEL), lambda i, j, te, tv: (i, 0)),
        scratch_shapes=[
            pltpu.VMEM((tm, D_MODEL), F32),
            pltpu.VMEM((tm, D_MODEL), BF16),
            pltpu.VMEM((tm, D_MODEL), F32),
            pltpu.SemaphoreType.DMA,
        ],
    )
    return pl.pallas_call(
        functools.partial(_moe_kernel, tm=tm, nfc=nfc),
        grid_spec=grid_spec,
        out_shape=jax.ShapeDtypeStruct((n_tiles * tm, D_MODEL), F32),
        compiler_params=_cparams(("arbitrary", "arbitrary")),
        name="moe_experts",
    )(tile_expert, tile_valid, row_token.reshape(n_tiles, 1, tm), row_gate, x1, wg, wu, wd)


def _combine_kernel(pos_ref, x_ref, y_hbm, g_ref, b_ref, o_ref, ybuf, sem, *, tm):
    def row_copy(r):
        return pltpu.make_async_copy(y_hbm.at[pl.ds(pos_ref[0, r], 1)], ybuf.at[pl.ds(r, 1)], sem)

    def start(r, carry):
        row_copy(r).start()
        return carry

    lax.fori_loop(0, 2 * tm, start, 0, unroll=8)

    def wait(r, carry):
        row_copy(r).wait()
        return carry

    lax.fori_loop(0, 2 * tm, wait, 0, unroll=8)
    f = ybuf[0:tm, :] + ybuf[tm:2 * tm, :]
    o_ref[...] = _layernorm(ALPHA * x_ref[...] + f, g_ref[...], b_ref[...])


def _combine(x1, y_sorted, pos, g, b, tm=256):
    T = x1.shape[0]
    row = lambda i: (i, 0)
    return pl.pallas_call(
        functools.partial(_combine_kernel, tm=tm),
        grid=(T // tm,),
        in_specs=[
            pl.BlockSpec((None, 1, 2 * tm), lambda i: (i, 0, 0), memory_space=pltpu.SMEM),
            pl.BlockSpec((tm, D_MODEL), row),
            pl.BlockSpec(memory_space=pl.ANY),
            _const_spec(g.shape), _const_spec(b.shape),
        ],
        out_specs=pl.BlockSpec((tm, D_MODEL), row),
        out_shape=jax.ShapeDtypeStruct((T, D_MODEL), F32),
        scratch_shapes=[pltpu.VMEM((2 * tm, D_MODEL), F32), pltpu.SemaphoreType.DMA],
        compiler_params=_cparams(("arbitrary",)),
        name="moe_combine",
    )(pos, x1, y_sorted, g, b)


def _route_tables(route, tm_e, tm_c):
    T = route.shape[0]
    e_flat = route[:, 0:2].astype(jnp.int32).reshape(-1)
    g_flat = route[:, 2:4].reshape(-1)
    A = 2 * T
    order = jnp.argsort(e_flat, stable=True).astype(jnp.int32)
    rank = jnp.zeros((A,), jnp.int32).at[order].set(jnp.arange(A, dtype=jnp.int32))
    counts = jnp.sum(jax.nn.one_hot(e_flat, N_EXPERTS, dtype=jnp.int32), axis=0)
    start = jnp.cumsum(counts) - counts
    padded = ((counts + tm_e - 1) // tm_e) * tm_e
    pend = jnp.cumsum(padded)
    pstart = pend - padded
    n_tiles = A // tm_e + N_EXPERTS
    rows = jnp.arange(n_tiles * tm_e, dtype=jnp.int32)
    e_row = jnp.minimum(jnp.searchsorted(pend, rows, side="right"), N_EXPERTS - 1).astype(jnp.int32)
    local = rows - pstart[e_row]
    ok = jnp.logical_and(local < counts[e_row], rows < pend[-1])
    a_row = order[jnp.clip(start[e_row] + local, 0, A - 1)]
    row_token = jnp.where(ok, a_row // 2, 0).astype(jnp.int32)
    row_gate = jnp.where(ok, g_flat[a_row], 0.0).astype(F32).reshape(-1, 1)
    tile_rows = jnp.arange(n_tiles, dtype=jnp.int32) * tm_e
    tile_valid = (tile_rows < pend[-1]).astype(jnp.int32)
    last_e = e_row[jnp.maximum(pend[-1] - 1, 0)]
    tile_expert = jnp.where(tile_valid == 1, e_row[tile_rows], last_e).astype(jnp.int32)
    dest = (pstart[e_flat] + rank - start[e_flat]).reshape(T, 2)
    pos = dest.reshape(T // tm_c, tm_c, 2).transpose(0, 2, 1).reshape(T // tm_c, 1, 2 * tm_c)
    return tile_expert, tile_valid, row_token, row_gate, pos


def _rot_cols(w):
    half = MLA_ROPE // 2
    return jnp.concatenate([-w[..., half:], w[..., :half]], -1)


def _prep_even(p, j):
    w_in = p["ev_w_in"][j]
    c_rot = 2 * CONV_CH + MLA_Q_LORA + MLA_KV_LORA
    k_rot = w_in[:, c_rot:c_rot + MLA_ROPE]
    z64 = jnp.zeros((D_MODEL, MLA_NOPE), F32)
    z32 = jnp.zeros((D_MODEL, HEAD_PAD - MLA_NOPE - MLA_ROPE), F32)
    w_in2 = jnp.concatenate([w_in[:, :c_rot], z64, k_rot, z32, z64, _rot_cols(k_rot), z32], -1).astype(BF16)
    wq = p["mla_w_uq"][j].reshape(MLA_Q_LORA, MLA_HEADS, MLA_NOPE + MLA_ROPE)
    zq = jnp.zeros((MLA_Q_LORA, MLA_HEADS, HEAD_PAD - MLA_NOPE - MLA_ROPE), F32)
    zq64 = jnp.zeros((MLA_Q_LORA, MLA_HEADS, MLA_NOPE), F32)
    wq_plain = jnp.concatenate([wq, zq], -1).reshape(MLA_Q_LORA, -1)
    wq_rot = jnp.concatenate([zq64, _rot_cols(wq[..., MLA_NOPE:]), zq], -1).reshape(MLA_Q_LORA, -1)
    wq2 = jnp.concatenate([wq_plain, wq_rot], -1).astype(BF16)
    wkv = p["mla_w_ukv"][j].reshape(MLA_KV_LORA, MLA_HEADS, MLA_NOPE + MLA_V)
    zk = jnp.zeros((MLA_KV_LORA, MLA_HEADS, MLA_V), F32)
    wk = jnp.concatenate([wkv[..., :MLA_NOPE], zk], -1).reshape(MLA_KV_LORA, -1)
    wv = wkv[..., MLA_NOPE:]
    even = (jnp.arange(MLA_HEADS) % 2 == 0)[None, :, None]
    wv2 = jnp.concatenate([jnp.where(even, wv, 0.0), jnp.where(even, 0.0, wv)], -1).reshape(MLA_KV_LORA, -1)
    wkv2 = jnp.concatenate([wk, wv2], -1).astype(BF16)
    w_out = p["ev_w_out"][j].astype(BF16)
    return dict(
        w_in=w_in2, qg=p["mla_q_norm_g"][j][None], wq=wq2, kvg=p["mla_kv_norm_g"][j][None], wkv=wkv2,
        dw_w=p["conv_dw_w"][j], dw_b=p["conv_dw_b"][j][None], cln_g=p["conv_ln_g"][j][None],
        cln_b=p["conv_ln_b"][j][None], w1=w_out[:CONV_CH], w2=w_out[CONV_CH:],
        wg=p["ffn_w_gate"][j].astype(BF16), wu=p["ffn_w_up"][j].astype(BF16), wd=p["ffn_w_down"][j].astype(BF16),
    )


def _lane_row(vals, lane0):
    return jnp.zeros((1, 128), F32).at[0, lane0:lane0 + vals.shape[0]].set(vals)


def _prep_odd(p, j):
    w = p["od_w_in"][j]
    small = jnp.concatenate([w[:, 1536:1552], w[:, 3600:3616], w[:, 3616:3632],
                             jnp.zeros((D_MODEL, 128 - 48), F32)], -1)
    w_in2 = jnp.concatenate([w[:, 0:1536], w[:, 1552:3600], small], -1).astype(BF16)
    w_out = p["od_w_out"][j].astype(BF16)
    rw = jnp.concatenate([p["moe_router_w"][j], jnp.zeros((D_MODEL, 128 - N_EXPERTS), F32)], -1).astype(BF16)
    rb = jnp.full((1, 128), -jnp.inf, F32).at[0, :N_EXPERTS].set(p["moe_router_b"][j])
    a = -jnp.exp(p["ssd_a_log"][j])
    return dict(
        w_in=w_in2, cw=p["ssd_conv_w"][j], cb=p["ssd_conv_b"][j][None],
        dt_bias=[_lane_row(p["ssd_dt_bias"][j][d], SM_DT + d * SSD_HEADS) for d in range(2)],
        a=[_lane_row(a[d], SM_DT + d * SSD_HEADS) for d in range(2)],
        gate_bias=[_lane_row(p["ml_igate_b"][j][d], SM_IG + d * ML_HEADS)
                   + _lane_row(p["ml_fgate_b"][j][d], SM_FG + d * ML_HEADS) for d in range(2)],
        dsk=jnp.repeat(p["ssd_d"][j], SSD_HEAD_DIM)[None], sg=p["ssd_norm_g"][j][None], mg=p["ml_norm_g"][j][None],
        w1=w_out[:SSD_INNER], w2=w_out[SSD_INNER:], rw=rw, rb=rb,
        wg=p["moe_w_gate"][j].astype(BF16), wu=p["moe_w_up"][j].astype(BF16), wd=p["moe_w_down"][j].astype(BF16),
    )


def _rope_tables(seq):
    half = MLA_ROPE // 2
    inv_freq = ROPE_THETA ** (-jnp.arange(half, dtype=F32) / half)
    ang = jnp.arange(seq, dtype=F32)[:, None] * inv_freq
    cos2 = jnp.concatenate([jnp.cos(ang), jnp.cos(ang)], -1)
    sin2 = jnp.concatenate([jnp.sin(ang), jnp.sin(ang)], -1)
    pad = jnp.zeros((seq, HEAD_PAD - MLA_NOPE - MLA_ROPE), F32)
    cos_t = jnp.concatenate([jnp.ones((seq, MLA_NOPE), F32), cos2, pad], -1)
    sin_t = jnp.concatenate([jnp.zeros((seq, MLA_NOPE), F32), sin2, pad], -1)
    return cos_t, sin_t


def _even_layer(x2, B, S, w, ln, cos_t, sin_t):
    tm = min(512, S)
    u, q, k, v = _ev_in(x2, cos_t, sin_t, w["w_in"], w["qg"], w["wq"], w["kvg"], w["wkv"], S, tm=tm)
    uc = _dwconv(u.reshape(B, S, CONV_CH), 0, w["dw_w"], w["dw_b"], w["cln_g"], w["cln_b"],
                 with_ln=True, out_dtype=BF16)
    att = _attention(q, k, v, B, S, tq=min(256, S))
    x1 = _ev_out(x2, uc.reshape(B * S, CONV_CH), att, w["w1"], w["w2"], ln[0], ln[1], tm=tm)
    return _ffn(x1, w["wg"], w["wu"], w["wd"], ln[2], ln[3], tm=tm)


def _odd_layer(x2, B, S, w, ln, tm_e=512, tm_c=256):
    T = B * S
    proj = _od_in(x2, w["w_in"], tm=min(512, S))
    proj3 = proj.reshape(B, S, OD_COLS)
    zeros = jnp.zeros((1, SSD_XBC), F32)
    xbc3 = _dwconv(proj3, OD_XBC // 512, w["cw"], w["cb"], zeros, zeros, with_ln=False, out_dtype=F32, ncb=2)
    yf = _ssd(xbc3, proj3, w["dt_bias"][0], w["a"][0], reverse=False, direction=0)
    yb = _ssd(xbc3, proj3, w["dt_bias"][1], w["a"][1], reverse=True, direction=1)
    hf = _mlstm(proj3, w["gate_bias"][0], reverse=False, direction=0)
    hb = _mlstm(proj3, w["gate_bias"][1], reverse=True, direction=1)
    flat = lambda a: a.reshape(T, -1)
    x1, route = _od_out(x2, proj, flat(xbc3), flat(yf), flat(yb), flat(hf), flat(hb), w["dsk"], w["sg"], w["mg"],
                        w["w1"], w["w2"], ln[0], ln[1], w["rw"], w["rb"])
    te, tv, row_token, row_gate, pos = _route_tables(route, tm_e, tm_c)
    y_sorted = _moe(x1, te, tv, row_token, row_gate, w["wg"], w["wu"], w["wd"], tm_e)
    return _combine(x1, y_sorted, pos, ln[2], ln[3], tm_c)


def _trunk(x, p):
    B, S, _ = x.shape
    x2 = x.reshape(B * S, D_MODEL)
    cos_t, sin_t = _rope_tables(S)
    for l in range(DEPTH):
        j = l // 2
        ln = (p["ln1_g"][l][None], p["ln1_b"][l][None], p["ln2_g"][l][None], p["ln2_b"][l][None])
        if l % 2 == 0:
            x2 = _even_layer(x2, B, S, _prep_even(p, j), ln, cos_t, sin_t)
        else:
            x2 = _odd_layer(x2, B, S, _prep_odd(p, j), ln)
    return x2.reshape(B, S, D_MODEL)


def kernel(x_prompt, x_sample, ev_w_in, conv_dw_w, conv_dw_b, conv_ln_g, conv_ln_b, mla_q_norm_g, mla_w_uq, mla_kv_norm_g, mla_w_ukv, ev_w_out, od_w_in, ssd_conv_w, ssd_conv_b, ssd_dt_bias, ssd_a_log, ssd_d, ssd_norm_g, ml_igate_b, ml_fgate_b, ml_norm_g, od_w_out, ffn_w_gate, ffn_w_up, ffn_w_down, moe_router_w, moe_router_b, moe_w_gate, moe_w_up, moe_w_down, ln1_g, ln1_b, ln2_g, ln2_b):
    p = dict(ev_w_in=ev_w_in, conv_dw_w=conv_dw_w, conv_dw_b=conv_dw_b, conv_ln_g=conv_ln_g, conv_ln_b=conv_ln_b,
             mla_q_norm_g=mla_q_norm_g, mla_w_uq=mla_w_uq, mla_kv_norm_g=mla_kv_norm_g, mla_w_ukv=mla_w_ukv,
             ev_w_out=ev_w_out, od_w_in=od_w_in, ssd_conv_w=ssd_conv_w, ssd_conv_b=ssd_conv_b,
             ssd_dt_bias=ssd_dt_bias, ssd_a_log=ssd_a_log, ssd_d=ssd_d, ssd_norm_g=ssd_norm_g,
             ml_igate_b=ml_igate_b, ml_fgate_b=ml_fgate_b, ml_norm_g=ml_norm_g, od_w_out=od_w_out,
             ffn_w_gate=ffn_w_gate, ffn_w_up=ffn_w_up, ffn_w_down=ffn_w_down, moe_router_w=moe_router_w,
             moe_router_b=moe_router_b, moe_w_gate=moe_w_gate, moe_w_up=moe_w_up, moe_w_down=moe_w_down,
             ln1_g=ln1_g, ln1_b=ln1_b, ln2_g=ln2_g, ln2_b=ln2_b)
    assert x_prompt.shape[1] == x_sample.shape[1]
    nb = x_prompt.shape[0]
    y = _trunk(jnp.concatenate([x_prompt, x_sample], 0), p)
    return (y[:nb], y[nb:])
```

```python
import functools
import math

import jax
import jax.numpy as jnp
import numpy as np
from jax import lax
from jax.experimental import pallas as pl
from jax.experimental.pallas import tpu as pltpu

F32 = jnp.float32
BF16 = jnp.bfloat16

D_MODEL = 1024
DEPTH = 4
ALPHA = (2.0 * DEPTH) ** 0.25
LN_EPS = 1e-5
RMS_EPS = 1e-6

CONV_CH = 512
CONV_W = 31
MLA_HEADS = 8
MLA_NOPE = 64
MLA_ROPE = 32
MLA_V = 64
MLA_Q_LORA = 256
MLA_KV_LORA = 128
ROPE_THETA = 10000.0
HEAD_PAD = 128
EV_COLS = 2 * CONV_CH + MLA_Q_LORA + MLA_KV_LORA + 2 * HEAD_PAD
Q_SCALE = (MLA_NOPE + MLA_ROPE) ** -0.5 * math.log2(math.e)

SSD_HEADS = 8
SSD_HEAD_DIM = 64
SSD_INNER = 512
SSD_GROUPS = 2
SSD_STATE = 128
SSD_CONV_W = 5
SSD_XBC = 1024
CHUNK = 128
ML_HEADS = 8
ML_HEAD_DIM = 64
ML_INNER = 512
OD_Z, OD_XBC, OD_Q, OD_V, OD_O, OD_DT, OD_IG, OD_FG = 0, 512, 1536, 2048, 2560, 3072, 3200, 3328
OD_COLS = 3456

D_FF = 2816
N_EXPERTS = 8
D_FF_EXPERT = 3584

VMEM_LIMIT = 56 * 1024 * 1024


def _cparams(sem):
    return pltpu.CompilerParams(dimension_semantics=sem, vmem_limit_bytes=VMEM_LIMIT)


def _const_spec(shape):
    nd = len(shape)
    return pl.BlockSpec(shape, lambda *_: (0,) * nd, pipeline_mode=pl.Buffered(1))


def _layernorm(v, g, b):
    mu = jnp.mean(v, -1, keepdims=True)
    d = v - mu
    var = jnp.mean(d * d, -1, keepdims=True)
    return d * lax.rsqrt(var + LN_EPS) * g + b


def _rmsnorm(v, g):
    return v * lax.rsqrt(jnp.mean(v * v, -1, keepdims=True) + RMS_EPS) * g


def _silu(v):
    return v * jax.nn.sigmoid(v)


def _dot(a, b):
    return jnp.dot(a, b, preferred_element_type=F32)


def _dot_nt(a, b):
    return lax.dot_general(a, b, (((1,), (1,)), ((), ())), preferred_element_type=F32)


def _ev_in_kernel(x_ref, cos_ref, sin_ref, w_in_ref, qg_ref, wq_ref, kvg_ref, wkv_ref,
                  u_ref, q_ref, k_ref, v_ref):
    xb = x_ref[...].astype(BF16)
    h = _dot(xb, w_in_ref[...])
    u_ref[...] = h[:, :CONV_CH] * jax.nn.sigmoid(h[:, CONV_CH:2 * CONV_CH])
    c0 = 2 * CONV_CH
    cos = cos_ref[...]
    sin = sin_ref[...]
    ql = _rmsnorm(h[:, c0:c0 + MLA_Q_LORA], qg_ref[...]).astype(BF16)
    qq = _dot(ql, wq_ref[...])
    c1 = c0 + MLA_Q_LORA
    kvl = _rmsnorm(h[:, c1:c1 + MLA_KV_LORA], kvg_ref[...]).astype(BF16)
    kk = _dot(kvl, wkv_ref[...])
    c2 = c1 + MLA_KV_LORA
    kpe = h[:, c2:c2 + HEAD_PAD] * cos + h[:, c2 + HEAD_PAD:c2 + 2 * HEAD_PAD] * sin
    nh = MLA_HEADS * HEAD_PAD
    for hd in range(MLA_HEADS):
        sl = slice(hd * HEAD_PAD, (hd + 1) * HEAD_PAD)
        sl2 = slice(nh + hd * HEAD_PAD, nh + (hd + 1) * HEAD_PAD)
        q_ref[:, sl] = ((qq[:, sl] * cos + qq[:, sl2] * sin) * Q_SCALE).astype(BF16)
        k_ref[:, sl] = (kk[:, sl] + kpe).astype(BF16)
    v_ref[...] = kk[:, nh:].astype(BF16)


def _ev_in(x2, cos_t, sin_t, w_in, qg, wq, kvg, wkv, seq, tm=512):
    T = x2.shape[0]
    nps = seq // tm
    row = lambda i: (i, 0)
    pos = lambda i: (i % nps, 0)
    nh = MLA_HEADS * HEAD_PAD
    return pl.pallas_call(
        _ev_in_kernel,
        grid=(T // tm,),
        in_specs=[
            pl.BlockSpec((tm, D_MODEL), row),
            pl.BlockSpec((tm, HEAD_PAD), pos),
            pl.BlockSpec((tm, HEAD_PAD), pos),
            _const_spec(w_in.shape), _const_spec(qg.shape), _const_spec(wq.shape),
            _const_spec(kvg.shape), _const_spec(wkv.shape),
        ],
        out_specs=[
            pl.BlockSpec((tm, CONV_CH), row),
            pl.BlockSpec((tm, nh), row),
            pl.BlockSpec((tm, nh), row),
            pl.BlockSpec((tm, nh), row),
        ],
        out_shape=[
            jax.ShapeDtypeStruct((T, CONV_CH), F32),
            jax.ShapeDtypeStruct((T, nh), BF16),
            jax.ShapeDtypeStruct((T, nh), BF16),
            jax.ShapeDtypeStruct((T, nh), BF16),
        ],
        compiler_params=_cparams(("parallel",)),
        name="ev_in",
    )(x2, cos_t, sin_t, w_in, qg, wq, kvg, wkv)


def _dwconv_kernel(x_ref, w_ref, b_ref, g_ref, beta_ref, o_ref, pad_ref, tmp_ref, *, width, halo, rows, with_ln):
    S, C = x_ref.shape
    half = width // 2
    win = rows + 2 * halo
    pad_ref[0:halo, :] = jnp.zeros((halo, C), F32)
    pad_ref[halo + S:halo + S + halo, :] = jnp.zeros((halo, C), F32)
    pad_ref[halo:halo + S, :] = x_ref[...]

    def tile(t, carry):
        r0 = pl.multiple_of(t * rows, rows)
        for cb in range(C // 128):
            cs = slice(cb * 128, (cb + 1) * 128)
            window = pad_ref[pl.ds(r0, win), cs]
            acc = jnp.zeros((rows, 128), F32) + b_ref[:, cs]
            for w in range(width):
                off = halo + w - half
                shifted = window if off == 0 else pltpu.roll(window, win - off, 0)
                acc = acc + shifted[:rows] * w_ref[w:w + 1, cs]
            tmp_ref[:, cs] = acc
        acc = tmp_ref[...]
        if with_ln:
            acc = _layernorm(acc, g_ref[...], beta_ref[...])
        o_ref[pl.ds(r0, rows), :] = _silu(acc).astype(o_ref.dtype)
        return carry

    lax.fori_loop(0, S // rows, tile, 0)


def _dwconv(x3, col_block, w, b, g, beta, *, with_ln, out_dtype, ncb=1, rows=128):
    B, S, _ = x3.shape
    C = 512
    width = w.shape[0]
    halo = 16
    assert width // 2 <= halo
    kern = functools.partial(_dwconv_kernel, width=width, halo=halo, rows=rows, with_ln=with_ln)
    return pl.pallas_call(
        kern,
        grid=(B, ncb),
        in_specs=[
            pl.BlockSpec((None, S, C), lambda b, c: (b, 0, col_block + c)),
            pl.BlockSpec((width, C), lambda b, c: (0, c)),
            pl.BlockSpec((1, C), lambda b, c: (0, c)),
            pl.BlockSpec((1, C), lambda b, c: (0, c)),
            pl.BlockSpec((1, C), lambda b, c: (0, c)),
        ],
        out_specs=pl.BlockSpec((None, S, C), lambda b, c: (b, 0, c)),
        out_shape=jax.ShapeDtypeStruct((B, S, C * ncb), out_dtype),
        scratch_shapes=[pltpu.VMEM((S + 2 * halo, C), F32), pltpu.VMEM((rows, C), F32)],
        compiler_params=_cparams(("parallel", "parallel")),
        name="dwconv_ln" if with_ln else "dwconv",
    )(x3, w, b, g, beta)


ATT_HEADS_PER_STEP = 4


def _attn_kernel(q_ref, k_ref, v_ref, o_ref):
    for pair in range(ATT_HEADS_PER_STEP // 2):
        acc = None
        for j in range(2):
            hd = 2 * pair + j
            sl = slice(hd * HEAD_PAD, (hd + 1) * HEAD_PAD)
            s = _dot_nt(q_ref[:, sl], k_ref[:, sl])
            m = jnp.max(s, -1, keepdims=True)
            p = jnp.exp2(s - m)
            l = jnp.sum(p, -1, keepdims=True)
            o = _dot(p.astype(BF16), v_ref[:, sl]) / l
            acc = o if acc is None else acc + o
        o_ref[:, pair * 2 * MLA_V:(pair + 1) * 2 * MLA_V] = acc.astype(o_ref.dtype)


def _attention(q, k, v, B, S, tq=256):
    nq = S // tq
    T = B * S
    k3 = k.reshape(B, S, -1)
    v3 = v.reshape(B, S, -1)
    hps = ATT_HEADS_PER_STEP
    return pl.pallas_call(
        _attn_kernel,
        grid=(B, MLA_HEADS // hps, nq),
        in_specs=[
            pl.BlockSpec((tq, hps * HEAD_PAD), lambda b, hp, i: (b * nq + i, hp)),
            pl.BlockSpec((None, S, hps * HEAD_PAD), lambda b, hp, i: (b, 0, hp)),
            pl.BlockSpec((None, S, hps * HEAD_PAD), lambda b, hp, i: (b, 0, hp)),
        ],
        out_specs=pl.BlockSpec((tq, hps * MLA_V), lambda b, hp, i: (b * nq + i, hp)),
        out_shape=jax.ShapeDtypeStruct((T, MLA_HEADS * MLA_V), BF16),
        compiler_params=_cparams(("parallel", "parallel", "parallel")),
        name="attention",
    )(q, k3, v3)


def _ev_out_kernel(x_ref, u_ref, a_ref, w1_ref, w2_ref, g_ref, b_ref, o_ref):
    m = _dot(u_ref[...], w1_ref[...]) + _dot(a_ref[...], w2_ref[...])
    o_ref[...] = _layernorm(ALPHA * x_ref[...] + m, g_ref[...], b_ref[...])


def _ev_out(x2, u, att, w1, w2, g, b, tm=512):
    T = x2.shape[0]
    row = lambda i: (i, 0)
    return pl.pallas_call(
        _ev_out_kernel,
        grid=(T // tm,),
        in_specs=[
            pl.BlockSpec((tm, D_MODEL), row),
            pl.BlockSpec((tm, CONV_CH), row),
            pl.BlockSpec((tm, MLA_HEADS * MLA_V), row),
            _const_spec(w1.shape), _const_spec(w2.shape), _const_spec(g.shape), _const_spec(b.shape),
        ],
        out_specs=pl.BlockSpec((tm, D_MODEL), row),
        out_shape=jax.ShapeDtypeStruct((T, D_MODEL), F32),
        compiler_params=_cparams(("parallel",)),
        name="ev_out",
    )(x2, u, att, w1, w2, g, b)


def _ffn_kernel(x_ref, wg_ref, wu_ref, wd_ref, g_ref, b_ref, o_ref, *, fc):
    x = x_ref[...]
    xb = x.astype(BF16)
    acc = jnp.zeros(x.shape, F32)
    for c in range(wg_ref.shape[1] // fc):
        sl = slice(c * fc, (c + 1) * fc)
        hh = _silu(_dot(xb, wg_ref[:, sl])) * _dot(xb, wu_ref[:, sl])
        acc = acc + _dot(hh.astype(BF16), wd_ref[sl, :])
    o_ref[...] = _layernorm(ALPHA * x + acc, g_ref[...], b_ref[...])


def _ffn(x2, wg, wu, wd, g, b, tm=512, fc=256):
    T = x2.shape[0]
    row = lambda i: (i, 0)
    return pl.pallas_call(
        functools.partial(_ffn_kernel, fc=fc),
        grid=(T // tm,),
        in_specs=[
            pl.BlockSpec((tm, D_MODEL), row),
            _const_spec(wg.shape), _const_spec(wu.shape), _const_spec(wd.shape),
            _const_spec(g.shape), _const_spec(b.shape),
        ],
        out_specs=pl.BlockSpec((tm, D_MODEL), row),
        out_shape=jax.ShapeDtypeStruct((T, D_MODEL), F32),
        compiler_params=_cparams(("parallel",)),
        name="ffn",
    )(x2, wg, wu, wd, g, b)


def _od_in_kernel(x_ref, w_ref, wkt_ref, o_ref, kt_ref):
    xb = x_ref[...].astype(BF16)
    o_ref[...] = _dot(xb, w_ref[...])
    kt_ref[...] = _dot_nt(wkt_ref[...], xb) * (ML_HEAD_DIM ** -0.5)


def _od_in(x2, w, wkt, seq, tm=512):
    T = x2.shape[0]
    nps = seq // tm
    row = lambda i: (i, 0)
    return pl.pallas_call(
        _od_in_kernel,
        grid=(T // tm,),
        in_specs=[pl.BlockSpec((tm, D_MODEL), row), _const_spec(w.shape), _const_spec(wkt.shape)],
        out_specs=[pl.BlockSpec((tm, OD_COLS), row),
                   pl.BlockSpec((None, ML_INNER, tm), lambda i: (i // nps, 0, i % nps))],
        out_shape=[jax.ShapeDtypeStruct((T, OD_COLS), F32),
                   jax.ShapeDtypeStruct((T // seq, ML_INNER, seq), F32)],
        compiler_params=_cparams(("parallel",)),
        name="od_in",
    )(x2, w, wkt)


def _tri(reverse):
    i = lax.broadcasted_iota(jnp.int32, (CHUNK, CHUNK), 0)
    j = lax.broadcasted_iota(jnp.int32, (CHUNK, CHUNK), 1)
    mask = (j >= i) if reverse else (j <= i)
    return mask, mask.astype(F32)


def _softplus(v):
    return jnp.maximum(v, 0.0) + jnp.log1p(jnp.exp(-jnp.abs(v)))


def _ssd_kernel(xbc_ref, sm_ref, bias_ref, a_ref, y_ref, st_ref, *, reverse, lane0):
    c = pl.program_id(1)

    @pl.when(c == 0)
    def _():
        st_ref[...] = jnp.zeros(st_ref.shape, F32)

    mask, tri = _tri(reverse)
    last = 0 if reverse else CHUNK - 1
    dt_all = _softplus(sm_ref[...] + bias_ref[...])
    da_all = dt_all * a_ref[...]
    cs_all = jnp.dot(tri, da_all, preferred_element_type=F32, precision=lax.Precision.HIGHEST)
    cs_t = cs_all.T
    tot_all = cs_all[last:last + 1, :]
    grow_all = jnp.exp(cs_all)
    rest_all = jnp.exp(tot_all - cs_all)
    etot_all = jnp.exp(tot_all)
    P = SSD_HEAD_DIM
    lo = lax.broadcasted_iota(jnp.int32, (CHUNK, 2 * P), 1) < P
    lo_row = lo[0:1, :]
    pairs_per_group = SSD_HEADS // SSD_GROUPS // 2
    prev = [st_ref[p] for p in range(SSD_HEADS // 2)]

    def pick(arr, l0):
        return jnp.where(lo if arr.shape[0] > 1 else lo_row, arr[:, l0:l0 + 1], arr[:, l0 + 1:l0 + 2])

    for g in range(SSD_GROUPS):
        b0 = SSD_INNER + g * SSD_STATE
        c0 = SSD_INNER + SSD_GROUPS * SSD_STATE + g * SSD_STATE
        cm = xbc_ref[:, c0:c0 + SSD_STATE].astype(BF16)
        bm_t = xbc_ref[:, b0:b0 + SSD_STATE].T.astype(BF16)
        cb = _dot(cm, bm_t)
        for pp in range(pairs_per_group):
            p = g * pairs_per_group + pp
            l0 = lane0 + 2 * p
            ps = slice(p * 2 * P, (p + 1) * 2 * P)
            xdt = xbc_ref[:, ps] * pick(dt_all, l0)
            x_lo = jnp.where(lo, xdt, 0.0).astype(BF16)
            x_hi = jnp.where(lo, 0.0, xdt).astype(BF16)
            dec0 = jnp.exp(jnp.where(mask, cs_all[:, l0:l0 + 1] - cs_t[l0:l0 + 1, :], -jnp.inf))
            dec1 = jnp.exp(jnp.where(mask, cs_all[:, l0 + 1:l0 + 2] - cs_t[l0 + 1:l0 + 2, :], -jnp.inf))
            y_diag = _dot((cb * dec0).astype(BF16), x_lo) + _dot((cb * dec1).astype(BF16), x_hi)
            y_off = _dot(cm, prev[p].astype(BF16)) * pick(grow_all, l0)
            y_ref[:, ps] = y_diag + y_off
            xw = (xdt * pick(rest_all, l0)).astype(BF16)
            st_ref[p] = pick(etot_all, l0) * prev[p] + _dot(bm_t, xw)


def _ssd(xbc3, proj3, bias_row, a_row, *, reverse, direction):
    B, S, _ = xbc3.shape
    nc = S // CHUNK
    cidx = (lambda c: nc - 1 - c) if reverse else (lambda c: c)
    kern = functools.partial(_ssd_kernel, reverse=reverse, lane0=direction * SSD_HEADS)
    return pl.pallas_call(
        kern,
        grid=(B, nc),
        in_specs=[
            pl.BlockSpec((None, CHUNK, SSD_XBC), lambda b, c: (b, cidx(c), 0)),
            pl.BlockSpec((None, CHUNK, 128), lambda b, c: (b, cidx(c), OD_DT // 128)),
            _const_spec(bias_row.shape), _const_spec(a_row.shape),
        ],
        out_specs=pl.BlockSpec((None, CHUNK, SSD_INNER), lambda b, c: (b, cidx(c), 0)),
        out_shape=jax.ShapeDtypeStruct((B, S, SSD_INNER), F32),
        scratch_shapes=[pltpu.VMEM((SSD_HEADS // 2, SSD_STATE, 2 * SSD_HEAD_DIM), F32)],
        compiler_params=_cparams(("parallel", "arbitrary")),
        name="ssd_rev" if reverse else "ssd_fwd",
    )(xbc3, proj3, bias_row, a_row)


def _running_max(x, reverse):
    n = x.shape[0]
    row = lax.broadcasted_iota(jnp.int32, x.shape, 0)
    s = 1
    while s < n:
        if reverse:
            sh = jnp.where(row < n - s, pltpu.roll(x, n - s, 0), -jnp.inf)
        else:
            sh = jnp.where(row >= s, pltpu.roll(x, s, 0), -jnp.inf)
        x = jnp.maximum(x, sh)
        s *= 2
    return x


def _mlstm_kernel(q_ref, kt_ref, v_ref, ig_ref, fg_ref, ib_ref, fb_ref, h_ref, cn_ref, m_ref, *, reverse, direction):
    c = pl.program_id(1)

    @pl.when(c == 0)
    def _():
        cn_ref[...] = jnp.zeros(cn_ref.shape, F32)
        m_ref[...] = jnp.zeros(m_ref.shape, F32)

    L = CHUNK
    mask, tri = _tri(reverse)
    last = 0 if reverse else L - 1
    li = ig_ref[...] + ib_ref[...]
    pre = fg_ref[...] + fb_ref[...]
    lf = jnp.minimum(pre, 0.0) - jnp.log1p(jnp.exp(-jnp.abs(pre)))
    bc = jnp.dot(tri, lf, preferred_element_type=F32, precision=lax.Precision.HIGHEST)
    u = li - bc
    cm = _running_max(u, reverse)
    m_prev = m_ref[...]
    big_m = jnp.maximum(m_prev, cm)
    g = bc[last:last + 1, :]
    m_loc = g + cm[last:last + 1, :]
    u_t = u.T
    e_end_t = jnp.exp(g + u - m_loc).T
    m_new = jnp.maximum(g + m_prev, m_loc)
    a_old = jnp.exp(g + m_prev - m_new)
    a_new = jnp.exp(m_loc - m_new)
    w_inter = jnp.exp(m_prev - big_m)
    emt = jnp.exp(-(bc + big_m))
    m_ref[...] = m_new

    lane = lax.broadcasted_iota(jnp.int32, (L, 128), 1)
    lo = lane < ML_HEAD_DIM
    row = lax.broadcasted_iota(jnp.int32, (128, 2 * 128), 0)
    col = lax.broadcasted_iota(jnp.int32, (128, 2 * 128), 1)
    block_diag = (row < ML_HEAD_DIM) == ((col % 128) < ML_HEAD_DIM)
    row_lo = lax.broadcasted_iota(jnp.int32, (128, L), 0) < ML_HEAD_DIM
    ones = jnp.ones((L, 128), F32)
    npairs = ML_HEADS // 2
    prev = [cn_ref[p] for p in range(npairs)]
    for p in range(npairs):
        l0 = direction * ML_HEADS + 2 * p
        l1 = l0 + 1
        ps = slice(p * 128, (p + 1) * 128)
        qp = q_ref[:, ps]
        vp = v_ref[:, ps]
        kt = kt_ref[ps, :]
        ktb = kt.astype(BF16)
        q_lo = jnp.where(lo, qp, 0.0).astype(BF16)
        q_hi = jnp.where(lo, 0.0, qp).astype(BF16)
        vo_lo = jnp.concatenate([jnp.where(lo, vp, 0.0), jnp.where(lo, ones, 0.0)], -1).astype(BF16)
        vo_hi = jnp.concatenate([jnp.where(lo, 0.0, vp), jnp.where(lo, 0.0, ones)], -1).astype(BF16)
        w0 = jnp.where(mask, jnp.exp(u_t[l0:l0 + 1, :] - big_m[:, l0:l0 + 1]), 0.0)
        w1 = jnp.where(mask, jnp.exp(u_t[l1:l1 + 1, :] - big_m[:, l1:l1 + 1]), 0.0)
        a0 = (_dot(q_lo, ktb) * w0).astype(BF16)
        a1 = (_dot(q_hi, ktb) * w1).astype(BF16)
        wi = jnp.where(lo, w_inter[:, l0:l0 + 1], w_inter[:, l1:l1 + 1])
        wi2 = jnp.concatenate([wi, wi], -1)
        nd = _dot(a0, vo_lo) + _dot(a1, vo_hi) + wi2 * _dot(qp.astype(BF16), prev[p].astype(BF16))
        floor = jnp.where(lo, emt[:, l0:l0 + 1], emt[:, l1:l1 + 1])
        h_ref[:, ps] = nd[:, :128] / jnp.maximum(jnp.abs(nd[:, 128:]), floor)
        e_t = jnp.where(row_lo, e_end_t[l0:l0 + 1, :], e_end_t[l1:l1 + 1, :])
        kte = (kt * e_t).astype(BF16)
        vo = jnp.concatenate([vp, ones], -1).astype(BF16)
        s_loc = jnp.where(block_diag, _dot(kte, vo), 0.0)
        row2 = lax.broadcasted_iota(jnp.int32, (128, 1), 0) < ML_HEAD_DIM
        ao = jnp.where(row2, a_old[:, l0:l0 + 1], a_old[:, l1:l1 + 1])
        an = jnp.where(row2, a_new[:, l0:l0 + 1], a_new[:, l1:l1 + 1])
        cn_ref[p] = ao * prev[p] + an * s_loc


def _mlstm(proj3, kt3, ib_row, fb_row, *, reverse, direction):
    B, S, _ = proj3.shape
    nc = S // CHUNK
    cidx = (lambda c: nc - 1 - c) if reverse else (lambda c: c)
    kern = functools.partial(_mlstm_kernel, reverse=reverse, direction=direction)
    col = lambda cb: (lambda b, c: (b, cidx(c), cb))
    return pl.pallas_call(
        kern,
        grid=(B, nc),
        in_specs=[
            pl.BlockSpec((None, CHUNK, ML_INNER), col(OD_Q // 512)),
            pl.BlockSpec((None, ML_INNER, CHUNK), lambda b, c: (b, 0, cidx(c))),
            pl.BlockSpec((None, CHUNK, ML_INNER), col(OD_V // 512)),
            pl.BlockSpec((None, CHUNK, 128), col(OD_IG // 128)),
            pl.BlockSpec((None, CHUNK, 128), col(OD_FG // 128)),
            _const_spec(ib_row.shape), _const_spec(fb_row.shape),
        ],
        out_specs=pl.BlockSpec((None, CHUNK, ML_INNER), col(0)),
        out_shape=jax.ShapeDtypeStruct((B, S, ML_INNER), F32),
        scratch_shapes=[
            pltpu.VMEM((ML_HEADS // 2, 128, 256), F32),
            pltpu.VMEM((1, 128), F32),
        ],
        compiler_params=_cparams(("parallel", "arbitrary")),
        name="mlstm_rev" if reverse else "mlstm_fwd",
    )(proj3, kt3, proj3, proj3, proj3, ib_row, fb_row)


def _od_out_kernel(x_ref, z_ref, xs_ref, o_ref, yf_ref, yb_ref, hf_ref, hb_ref, dsk_ref, sg_ref, mg_ref, avg_ref,
                   w1_ref, w2_ref, g_ref, b_ref, rw_ref, rb_ref, x1_ref, route_ref):
    y = (yf_ref[...] + yb_ref[...] + xs_ref[...] * dsk_ref[...]) * _silu(z_ref[...])
    gw = SSD_INNER // SSD_GROUPS
    m = None
    for g in range(SSD_GROUPS):
        sl = slice(g * gw, (g + 1) * gw)
        yn = _rmsnorm(y[:, sl], sg_ref[:, sl]).astype(BF16)
        t = _dot(yn, w1_ref[sl, :])
        m = t if m is None else m + t
    hs = hf_ref[...] + hb_ref[...]
    avg = avg_ref[...]

    def head_mean(v):
        hi = v.astype(BF16)
        lo = (v - hi.astype(F32)).astype(BF16)
        return _dot(hi, avg) + _dot(lo, avg)

    dv = hs - head_mean(hs)
    var = head_mean(dv * dv)
    hn = jax.nn.sigmoid(o_ref[...]) * (dv * lax.rsqrt(var + LN_EPS) * mg_ref[...])
    m = m + _dot(hn.astype(BF16), w2_ref[...])
    x1 = _layernorm(ALPHA * x_ref[...] + m, g_ref[...], b_ref[...])
    x1_ref[...] = x1
    logits = _dot(x1.astype(BF16), rw_ref[...]) + rb_ref[...]
    lane = lax.broadcasted_iota(jnp.int32, logits.shape, 1).astype(F32)
    m1 = jnp.max(logits, -1, keepdims=True)
    i1 = jnp.min(jnp.where(logits == m1, lane, 128.0), -1, keepdims=True)
    rest = jnp.where(lane == i1, -jnp.inf, logits)
    m2 = jnp.max(rest, -1, keepdims=True)
    i2 = jnp.min(jnp.where(rest == m2, lane, 128.0), -1, keepdims=True)
    e = jnp.exp(m2 - m1)
    g1 = 1.0 / (1.0 + e)
    g2 = e / (1.0 + e)
    route_ref[...] = jnp.where(lane == 0.0, i1,
                               jnp.where(lane == 1.0, i2, jnp.where(lane == 2.0, g1, jnp.where(lane == 3.0, g2, 0.0))))


def _od_out(x2, proj, xbc, yf, yb, hf, hb, dsk, sg, mg, w1, w2, g, b, rw, rb, tm=256):
    T = x2.shape[0]
    row = lambda i: (i, 0)
    colb = lambda cb: (lambda i: (i, cb))
    head = jnp.arange(ML_INNER) // ML_HEAD_DIM
    avg = jnp.where(head[:, None] == head[None, :], 1.0 / ML_HEAD_DIM, 0.0).astype(BF16)
    return pl.pallas_call(
        _od_out_kernel,
        grid=(T // tm,),
        in_specs=[
            pl.BlockSpec((tm, D_MODEL), row),
            pl.BlockSpec((tm, 512), colb(OD_Z // 512)),
            pl.BlockSpec((tm, 512), colb(0)),
            pl.BlockSpec((tm, 512), colb(OD_O // 512)),
            pl.BlockSpec((tm, 512), row), pl.BlockSpec((tm, 512), row),
            pl.BlockSpec((tm, 512), row), pl.BlockSpec((tm, 512), row),
            _const_spec(dsk.shape), _const_spec(sg.shape), _const_spec(mg.shape), _const_spec(avg.shape),
            _const_spec(w1.shape), _const_spec(w2.shape), _const_spec(g.shape), _const_spec(b.shape),
            _const_spec(rw.shape), _const_spec(rb.shape),
        ],
        out_specs=[pl.BlockSpec((tm, D_MODEL), row), pl.BlockSpec((tm, 128), row)],
        out_shape=[jax.ShapeDtypeStruct((T, D_MODEL), F32), jax.ShapeDtypeStruct((T, 128), F32)],
        compiler_params=_cparams(("parallel",)),
        name="od_out",
    )(x2, proj, xbc, proj, yf, yb, hf, hb, dsk, sg, mg, avg, w1, w2, g, b, rw, rb)


def _gather_rows(src_hbm, idx_smem, buf, sem, slot, rows):
    for r in rows:
        pltpu.make_async_copy(src_hbm.at[pl.ds(idx_smem[0, r], 1)], buf.at[slot, pl.ds(r, 1)], sem.at[slot]).start()


def _gather_wait(src_hbm, buf, sem, slot):
    n = buf.shape[1]
    pltpu.make_async_copy(src_hbm.at[pl.ds(0, n)], buf.at[slot], sem.at[slot]).wait()


def _gather_all(src_hbm, idx_smem, buf, sem, slot):
    def start(r, carry):
        _gather_rows(src_hbm, idx_smem, buf, sem, slot, [r])
        return carry

    lax.fori_loop(0, buf.shape[1], start, 0, unroll=8)


def _moe_kernel(te_ref, tok0_ref, tokn_ref, gate_ref, x_hbm, wg_ref, wu_ref, wd_ref, o_ref,
                xbuf, xb_ref, acc_ref, sem, *, tm, nfc, sub):
    i = pl.program_id(0)
    j = pl.program_id(1)
    n = pl.num_programs(0)
    slot = i % 2

    @pl.when(jnp.logical_and(i == 0, j == 0))
    def _():
        _gather_all(x_hbm, tok0_ref, xbuf, sem, 0)

    @pl.when(j == 0)
    def _():
        _gather_wait(x_hbm, xbuf, sem, slot)
        xb_ref[...] = xbuf[slot].astype(BF16)
        acc_ref[...] = jnp.zeros(acc_ref.shape, F32)

    per_step = tm // nfc
    _gather_rows(x_hbm, tokn_ref, xbuf, sem, 1 - slot, [j * per_step + r for r in range(per_step)])
    xb = xb_ref[...]
    acc = acc_ref[...]
    for c in range(wg_ref.shape[1] // sub):
        sl = slice(c * sub, (c + 1) * sub)
        hh = _silu(_dot(xb, wg_ref[:, sl])) * _dot(xb, wu_ref[:, sl])
        acc = acc + _dot(hh.astype(BF16), wd_ref[sl, :])
    acc_ref[...] = acc

    @pl.when(j == nfc - 1)
    def _():
        o_ref[...] = acc_ref[...] * gate_ref[...]

    @pl.when(jnp.logical_and(i == n - 1, j == nfc - 1))
    def _():
        _gather_wait(x_hbm, xbuf, sem, 1 - slot)


def _moe(x1, tile_expert, row_token, row_gate, wg, wu, wd, tm, fc=1792, sub=256):
    n_tiles = tile_expert.shape[0]
    nfc = D_FF_EXPERT // fc
    grid_spec = pltpu.PrefetchScalarGridSpec(
        num_scalar_prefetch=1,
        grid=(n_tiles, nfc),
        in_specs=[
            pl.BlockSpec((None, 1, tm), lambda i, j, te: (0, 0, 0), memory_space=pltpu.SMEM),
            pl.BlockSpec((None, 1, tm), lambda i, j, te: (i + 1, 0, 0), memory_space=pltpu.SMEM),
            pl.BlockSpec((tm, 1), lambda i, j, te: (i, 0)),
            pl.BlockSpec(memory_space=pl.ANY),
            pl.BlockSpec((None, D_MODEL, fc), lambda i, j, te: (te[i], 0, j)),
            pl.BlockSpec((None, D_MODEL, fc), lambda i, j, te: (te[i], 0, j)),
            pl.BlockSpec((None, fc, D_MODEL), lambda i, j, te: (te[i], j, 0)),
        ],
        out_specs=pl.BlockSpec((tm, D_MODEL), lambda i, j, te: (i, 0)),
        scratch_shapes=[
            pltpu.VMEM((2, tm, D_MODEL), F32),
            pltpu.VMEM((tm, D_MODEL), BF16),
            pltpu.VMEM((tm, D_MODEL), F32),
            pltpu.SemaphoreType.DMA((2,)),
        ],
    )
    tok3 = row_token.reshape(n_tiles + 1, 1, tm)
    return pl.pallas_call(
        functools.partial(_moe_kernel, tm=tm, nfc=nfc, sub=sub),
        grid_spec=grid_spec,
        out_shape=jax.ShapeDtypeStruct((n_tiles * tm, D_MODEL), F32),
        compiler_params=_cparams(("arbitrary", "arbitrary")),
        name="moe_experts",
    )(tile_expert, tok3, tok3, row_gate, x1, wg, wu, wd)


def _combine_kernel(pos0_ref, posn_ref, x_ref, y_hbm, g_ref, b_ref, o_ref, ybuf, sem, *, tm):
    i = pl.program_id(0)
    n = pl.num_programs(0)
    slot = i % 2

    @pl.when(i == 0)
    def _():
        _gather_all(y_hbm, pos0_ref, ybuf, sem, 0)

    _gather_rows(y_hbm, posn_ref, ybuf, sem, 1 - slot, range(2 * tm))
    _gather_wait(y_hbm, ybuf, sem, slot)
    f = ybuf[slot, 0:tm, :] + ybuf[slot, tm:2 * tm, :]
    o_ref[...] = _layernorm(ALPHA * x_ref[...] + f, g_ref[...], b_ref[...])

    @pl.when(i == n - 1)
    def _():
        _gather_wait(y_hbm, ybuf, sem, 1 - slot)


def _combine(x1, y_sorted, pos, g, b, tm=256):
    T = x1.shape[0]
    row = lambda i: (i, 0)
    return pl.pallas_call(
        functools.partial(_combine_kernel, tm=tm),
        grid=(T // tm,),
        in_specs=[
            pl.BlockSpec((None, 1, 2 * tm), lambda i: (0, 0, 0), memory_space=pltpu.SMEM),
            pl.BlockSpec((None, 1, 2 * tm), lambda i: (i + 1, 0, 0), memory_space=pltpu.SMEM),
            pl.BlockSpec((tm, D_MODEL), row),
            pl.BlockSpec(memory_space=pl.ANY),
            _const_spec(g.shape), _const_spec(b.shape),
        ],
        out_specs=pl.BlockSpec((tm, D_MODEL), row),
        out_shape=jax.ShapeDtypeStruct((T, D_MODEL), F32),
        scratch_shapes=[pltpu.VMEM((2, 2 * tm, D_MODEL), F32), pltpu.SemaphoreType.DMA((2,))],
        compiler_params=_cparams(("arbitrary",)),
        name="moe_combine",
    )(pos, pos, x1, y_sorted, g, b)


def _route_tables(route, tm_e, tm_c):
    T = route.shape[0]
    e_flat = route[:, 0:2].astype(jnp.int32).reshape(-1)
    g_flat = route[:, 2:4].reshape(-1)
    A = 2 * T
    order = jnp.argsort(e_flat, stable=True).astype(jnp.int32)
    onehot = jax.nn.one_hot(e_flat, N_EXPERTS, dtype=jnp.int32)
    before = jnp.cumsum(onehot, axis=0) - onehot
    counts = jnp.sum(onehot, axis=0)
    start = jnp.cumsum(counts) - counts
    rank = start[e_flat] + jnp.sum(before * onehot, axis=1)
    padded = ((counts + tm_e - 1) // tm_e) * tm_e
    pend = jnp.cumsum(padded)
    pstart = pend - padded
    n_tiles = A // tm_e + N_EXPERTS
    rows = jnp.arange(n_tiles * tm_e, dtype=jnp.int32)
    e_row = jnp.minimum(jnp.searchsorted(pend, rows, side="right"), N_EXPERTS - 1).astype(jnp.int32)
    local = rows - pstart[e_row]
    ok = jnp.logical_and(local < counts[e_row], rows < pend[-1])
    a_row = order[jnp.clip(start[e_row] + local, 0, A - 1)]
    row_token = jnp.where(ok, a_row // 2, 0).astype(jnp.int32)
    row_token = jnp.concatenate([row_token, jnp.zeros((tm_e,), jnp.int32)])
    row_gate = jnp.where(ok, g_flat[a_row], 0.0).astype(F32).reshape(-1, 1)
    tile_rows = jnp.arange(n_tiles, dtype=jnp.int32) * tm_e
    tile_expert = e_row[tile_rows]
    dest = (pstart[e_flat] + rank - start[e_flat]).reshape(T, 2)
    pos = dest.reshape(T // tm_c, tm_c, 2).transpose(0, 2, 1).reshape(T // tm_c, 1, 2 * tm_c)
    pos = jnp.concatenate([pos, jnp.zeros((1, 1, 2 * tm_c), jnp.int32)], 0).astype(jnp.int32)
    return tile_expert, row_token, row_gate, pos


def _rot_cols(w):
    half = MLA_ROPE // 2
    return jnp.concatenate([-w[..., half:], w[..., :half]], -1)


def _prep_even(p, j):
    w_in = p["ev_w_in"][j]
    c_rot = 2 * CONV_CH + MLA_Q_LORA + MLA_KV_LORA
    k_rot = w_in[:, c_rot:c_rot + MLA_ROPE]
    z64 = jnp.zeros((D_MODEL, MLA_NOPE), F32)
    z32 = jnp.zeros((D_MODEL, HEAD_PAD - MLA_NOPE - MLA_ROPE), F32)
    w_in2 = jnp.concatenate([w_in[:, :c_rot], z64, k_rot, z32, z64, _rot_cols(k_rot), z32], -1).astype(BF16)
    wq = p["mla_w_uq"][j].reshape(MLA_Q_LORA, MLA_HEADS, MLA_NOPE + MLA_ROPE)
    zq = jnp.zeros((MLA_Q_LORA, MLA_HEADS, HEAD_PAD - MLA_NOPE - MLA_ROPE), F32)
    zq64 = jnp.zeros((MLA_Q_LORA, MLA_HEADS, MLA_NOPE), F32)
    wq_plain = jnp.concatenate([wq, zq], -1).reshape(MLA_Q_LORA, -1)
    wq_rot = jnp.concatenate([zq64, _rot_cols(wq[..., MLA_NOPE:]), zq], -1).reshape(MLA_Q_LORA, -1)
    wq2 = jnp.concatenate([wq_plain, wq_rot], -1).astype(BF16)
    wkv = p["mla_w_ukv"][j].reshape(MLA_KV_LORA, MLA_HEADS, MLA_NOPE + MLA_V)
    zk = jnp.zeros((MLA_KV_LORA, MLA_HEADS, MLA_V), F32)
    wk = jnp.concatenate([wkv[..., :MLA_NOPE], zk], -1).reshape(MLA_KV_LORA, -1)
    wv = wkv[..., MLA_NOPE:]
    even = (jnp.arange(MLA_HEADS) % 2 == 0)[None, :, None]
    wv2 = jnp.concatenate([jnp.where(even, wv, 0.0), jnp.where(even, 0.0, wv)], -1).reshape(MLA_KV_LORA, -1)
    wkv2 = jnp.concatenate([wk, wv2], -1).astype(BF16)
    w_out = p["ev_w_out"][j].astype(BF16)
    return dict(
        w_in=w_in2, qg=p["mla_q_norm_g"][j][None], wq=wq2, kvg=p["mla_kv_norm_g"][j][None], wkv=wkv2,
        dw_w=p["conv_dw_w"][j], dw_b=p["conv_dw_b"][j][None], cln_g=p["conv_ln_g"][j][None],
        cln_b=p["conv_ln_b"][j][None], w1=w_out[:CONV_CH], w2=w_out[CONV_CH:],
        wg=p["ffn_w_gate"][j].astype(BF16), wu=p["ffn_w_up"][j].astype(BF16), wd=p["ffn_w_down"][j].astype(BF16),
    )


def _lane_row(vals, lane0):
    return jnp.zeros((1, 128), F32).at[0, lane0:lane0 + vals.shape[0]].set(vals)


def _prep_odd(p, j):
    w = p["od_w_in"][j]
    z112 = jnp.zeros((D_MODEL, 128 - 16), F32)
    w_in2 = jnp.concatenate([w[:, 0:1536], w[:, 1552:2064], w[:, 2576:3600],
                             w[:, 1536:1552], z112, w[:, 3600:3616], z112, w[:, 3616:3632], z112], -1).astype(BF16)
    wkt = w[:, 2064:2576].T.astype(BF16)
    w_out = p["od_w_out"][j].astype(BF16)
    rw = jnp.concatenate([p["moe_router_w"][j], jnp.zeros((D_MODEL, 128 - N_EXPERTS), F32)], -1).astype(BF16)
    rb = jnp.full((1, 128), -jnp.inf, F32).at[0, :N_EXPERTS].set(p["moe_router_b"][j])
    a = -jnp.exp(p["ssd_a_log"][j])
    return dict(
        w_in=w_in2, wkt=wkt, cw=p["ssd_conv_w"][j], cb=p["ssd_conv_b"][j][None],
        dt_bias=[_lane_row(p["ssd_dt_bias"][j][d], d * SSD_HEADS) for d in range(2)],
        a=[_lane_row(a[d], d * SSD_HEADS) for d in range(2)],
        ig_b=_lane_row(p["ml_igate_b"][j].reshape(-1), 0), fg_b=_lane_row(p["ml_fgate_b"][j].reshape(-1), 0),
        dsk=jnp.repeat(p["ssd_d"][j], SSD_HEAD_DIM)[None], sg=p["ssd_norm_g"][j][None], mg=p["ml_norm_g"][j][None],
        w1=w_out[:SSD_INNER], w2=w_out[SSD_INNER:], rw=rw, rb=rb,
        wg=p["moe_w_gate"][j].astype(BF16), wu=p["moe_w_up"][j].astype(BF16), wd=p["moe_w_down"][j].astype(BF16),
    )


def _rope_tables(seq):
    half = MLA_ROPE // 2
    inv_freq = ROPE_THETA ** (-jnp.arange(half, dtype=F32) / half)
    ang = jnp.arange(seq, dtype=F32)[:, None] * inv_freq
    cos2 = jnp.concatenate([jnp.cos(ang), jnp.cos(ang)], -1)
    sin2 = jnp.concatenate([jnp.sin(ang), jnp.sin(ang)], -1)
    pad = jnp.zeros((seq, HEAD_PAD - MLA_NOPE - MLA_ROPE), F32)
    cos_t = jnp.concatenate([jnp.ones((seq, MLA_NOPE), F32), cos2, pad], -1)
    sin_t = jnp.concatenate([jnp.zeros((seq, MLA_NOPE), F32), sin2, pad], -1)
    return cos_t, sin_t


def _even_layer(x2, B, S, w, ln, cos_t, sin_t):
    tm = min(512, S)
    u, q, k, v = _ev_in(x2, cos_t, sin_t, w["w_in"], w["qg"], w["wq"], w["kvg"], w["wkv"], S, tm=tm)
    uc = _dwconv(u.reshape(B, S, CONV_CH), 0, w["dw_w"], w["dw_b"], w["cln_g"], w["cln_b"],
                 with_ln=True, out_dtype=BF16)
    att = _attention(q, k, v, B, S, tq=min(256, S))
    x1 = _ev_out(x2, uc.reshape(B * S, CONV_CH), att, w["w1"], w["w2"], ln[0], ln[1], tm=tm)
    return _ffn(x1, w["wg"], w["wu"], w["wd"], ln[2], ln[3], tm=tm)


def _odd_layer(x2, B, S, w, ln, tm_e=512, tm_c=256):
    T = B * S
    proj, kt3 = _od_in(x2, w["w_in"], w["wkt"], S, tm=min(512, S))
    proj3 = proj.reshape(B, S, OD_COLS)
    zeros = jnp.zeros((1, SSD_XBC), F32)
    xbc3 = _dwconv(proj3, OD_XBC // 512, w["cw"], w["cb"], zeros, zeros, with_ln=False, out_dtype=F32, ncb=2)
    yf = _ssd(xbc3, proj3, w["dt_bias"][0], w["a"][0], reverse=False, direction=0)
    yb = _ssd(xbc3, proj3, w["dt_bias"][1], w["a"][1], reverse=True, direction=1)
    hf = _mlstm(proj3, kt3, w["ig_b"], w["fg_b"], reverse=False, direction=0)
    hb = _mlstm(proj3, kt3, w["ig_b"], w["fg_b"], reverse=True, direction=1)
    flat = lambda a: a.reshape(T, -1)
    x1, route = _od_out(x2, proj, flat(xbc3), flat(yf), flat(yb), flat(hf), flat(hb), w["dsk"], w["sg"], w["mg"],
                        w["w1"], w["w2"], ln[0], ln[1], w["rw"], w["rb"])
    te, row_token, row_gate, pos = _route_tables(route, tm_e, tm_c)
    y_sorted = _moe(x1, te, row_token, row_gate, w["wg"], w["wu"], w["wd"], tm_e)
    return _combine(x1, y_sorted, pos, ln[2], ln[3], tm_c)


def _trunk(x, p):
    B, S, _ = x.shape
    x2 = x.reshape(B * S, D_MODEL)
    cos_t, sin_t = _rope_tables(S)
    for l in range(DEPTH):
        j = l // 2
        ln = (p["ln1_g"][l][None], p["ln1_b"][l][None], p["ln2_g"][l][None], p["ln2_b"][l][None])
        if l % 2 == 0:
            x2 = _even_layer(x2, B, S, _prep_even(p, j), ln, cos_t, sin_t)
        else:
            x2 = _odd_layer(x2, B, S, _prep_odd(p, j), ln)
    return x2.reshape(B, S, D_MODEL)


def kernel(x_prompt, x_sample, ev_w_in, conv_dw_w, conv_dw_b, conv_ln_g, conv_ln_b, mla_q_norm_g, mla_w_uq, mla_kv_norm_g, mla_w_ukv, ev_w_out, od_w_in, ssd_conv_w, ssd_conv_b, ssd_dt_bias, ssd_a_log, ssd_d, ssd_norm_g, ml_igate_b, ml_fgate_b, ml_norm_g, od_w_out, ffn_w_gate, ffn_w_up, ffn_w_down, moe_router_w, moe_router_b, moe_w_gate, moe_w_up, moe_w_down, ln1_g, ln1_b, ln2_g, ln2_b):
    p = dict(ev_w_in=ev_w_in, conv_dw_w=conv_dw_w, conv_dw_b=conv_dw_b, conv_ln_g=conv_ln_g, conv_ln_b=conv_ln_b,
             mla_q_norm_g=mla_q_norm_g, mla_w_uq=mla_w_uq, mla_kv_norm_g=mla_kv_norm_g, mla_w_ukv=mla_w_ukv,
             ev_w_out=ev_w_out, od_w_in=od_w_in, ssd_conv_w=ssd_conv_w, ssd_conv_b=ssd_conv_b,
             ssd_dt_bias=ssd_dt_bias, ssd_a_log=ssd_a_log, ssd_d=ssd_d, ssd_norm_g=ssd_norm_g,
             ml_igate_b=ml_igate_b, ml_fgate_b=ml_fgate_b, ml_norm_g=ml_norm_g, od_w_out=od_w_out,
             ffn_w_gate=ffn_w_gate, ffn_w_up=ffn_w_up, ffn_w_down=ffn_w_down, moe_router_w=moe_router_w,
             moe_router_b=moe_router_b, moe_w_gate=moe_w_gate, moe_w_up=moe_w_up, moe_w_down=moe_w_down,
             ln1_g=ln1_g, ln1_b=ln1_b, ln2_g=ln2_g, ln2_b=ln2_b)
    assert x_prompt.shape[1] == x_sample.shape[1]
    nb = x_prompt.shape[0]
    y = _trunk(jnp.concatenate([x_prompt, x_sample], 0), p)
    return (y[:nb], y[nb:])
```

```python
import functools
import math

import jax
import jax.numpy as jnp
import numpy as np
from jax import lax
from jax.experimental import pallas as pl
from jax.experimental.pallas import tpu as pltpu

F32 = jnp.float32
BF16 = jnp.bfloat16

D_MODEL = 1024
DEPTH = 4
ALPHA = (2.0 * DEPTH) ** 0.25
LN_EPS = 1e-5
RMS_EPS = 1e-6

CONV_CH = 512
CONV_W = 31
MLA_HEADS = 8
MLA_NOPE = 64
MLA_ROPE = 32
MLA_V = 64
MLA_Q_LORA = 256
MLA_KV_LORA = 128
ROPE_THETA = 10000.0
HEAD_PAD = 128
EV_COLS = 2 * CONV_CH + MLA_Q_LORA + MLA_KV_LORA + 2 * HEAD_PAD
Q_SCALE = (MLA_NOPE + MLA_ROPE) ** -0.5 * math.log2(math.e)

SSD_HEADS = 8
SSD_HEAD_DIM = 64
SSD_INNER = 512
SSD_GROUPS = 2
SSD_STATE = 128
SSD_CONV_W = 5
SSD_XBC = 1024
CHUNK = 128
ML_HEADS = 8
ML_HEAD_DIM = 64
ML_INNER = 512
OD_Z, OD_XBC, OD_Q, OD_V, OD_O, OD_DT, OD_IG, OD_FG = 0, 512, 1536, 2048, 2560, 3072, 3200, 3328
OD_COLS = 3456

D_FF = 2816
N_EXPERTS = 8
D_FF_EXPERT = 3584

VMEM_LIMIT = 56 * 1024 * 1024


def _cparams(sem):
    return pltpu.CompilerParams(dimension_semantics=sem, vmem_limit_bytes=VMEM_LIMIT)


def _const_spec(shape):
    nd = len(shape)
    return pl.BlockSpec(shape, lambda *_: (0,) * nd, pipeline_mode=pl.Buffered(1))


def _layernorm(v, g, b):
    mu = jnp.mean(v, -1, keepdims=True)
    d = v - mu
    var = jnp.mean(d * d, -1, keepdims=True)
    return d * lax.rsqrt(var + LN_EPS) * g + b


def _rmsnorm(v, g):
    return v * lax.rsqrt(jnp.mean(v * v, -1, keepdims=True) + RMS_EPS) * g


def _silu(v):
    return v * jax.nn.sigmoid(v)


def _dot(a, b):
    return jnp.dot(a, b, preferred_element_type=F32)


def _dot_nt(a, b):
    return lax.dot_general(a, b, (((1,), (1,)), ((), ())), preferred_element_type=F32)


def _ev_in_kernel(x_ref, cos_ref, sin_ref, w_in_ref, qg_ref, wq_ref, kvg_ref, wkv_ref,
                  u_ref, q_ref, k_ref, v_ref):
    xb = x_ref[...].astype(BF16)
    h = _dot(xb, w_in_ref[...])
    u_ref[...] = h[:, :CONV_CH] * jax.nn.sigmoid(h[:, CONV_CH:2 * CONV_CH])
    c0 = 2 * CONV_CH
    cos = cos_ref[...]
    sin = sin_ref[...]
    ql = _rmsnorm(h[:, c0:c0 + MLA_Q_LORA], qg_ref[...]).astype(BF16)
    qq = _dot(ql, wq_ref[...])
    c1 = c0 + MLA_Q_LORA
    kvl = _rmsnorm(h[:, c1:c1 + MLA_KV_LORA], kvg_ref[...]).astype(BF16)
    kk = _dot(kvl, wkv_ref[...])
    c2 = c1 + MLA_KV_LORA
    kpe = h[:, c2:c2 + HEAD_PAD] * cos + h[:, c2 + HEAD_PAD:c2 + 2 * HEAD_PAD] * sin
    nh = MLA_HEADS * HEAD_PAD
    for hd in range(MLA_HEADS):
        sl = slice(hd * HEAD_PAD, (hd + 1) * HEAD_PAD)
        sl2 = slice(nh + hd * HEAD_PAD, nh + (hd + 1) * HEAD_PAD)
        q_ref[:, sl] = ((qq[:, sl] * cos + qq[:, sl2] * sin) * Q_SCALE).astype(BF16)
        k_ref[:, sl] = (kk[:, sl] + kpe).astype(BF16)
    v_ref[...] = kk[:, nh:].astype(BF16)


def _ev_in(x2, cos_t, sin_t, w_in, qg, wq, kvg, wkv, seq, tm=512):
    T = x2.shape[0]
    nps = seq // tm
    row = lambda i: (i, 0)
    pos = lambda i: (i % nps, 0)
    nh = MLA_HEADS * HEAD_PAD
    return pl.pallas_call(
        _ev_in_kernel,
        grid=(T // tm,),
        in_specs=[
            pl.BlockSpec((tm, D_MODEL), row),
            pl.BlockSpec((tm, HEAD_PAD), pos),
            pl.BlockSpec((tm, HEAD_PAD), pos),
            _const_spec(w_in.shape), _const_spec(qg.shape), _const_spec(wq.shape),
            _const_spec(kvg.shape), _const_spec(wkv.shape),
        ],
        out_specs=[
            pl.BlockSpec((tm, CONV_CH), row),
            pl.BlockSpec((tm, nh), row),
            pl.BlockSpec((tm, nh), row),
            pl.BlockSpec((tm, nh), row),
        ],
        out_shape=[
            jax.ShapeDtypeStruct((T, CONV_CH), F32),
            jax.ShapeDtypeStruct((T, nh), BF16),
            jax.ShapeDtypeStruct((T, nh), BF16),
            jax.ShapeDtypeStruct((T, nh), BF16),
        ],
        compiler_params=_cparams(("parallel",)),
        name="ev_in",
    )(x2, cos_t, sin_t, w_in, qg, wq, kvg, wkv)


def _dwconv_kernel(x_ref, w_ref, b_ref, g_ref, beta_ref, o_ref, pad_ref, tmp_ref, *, width, halo, rows, with_ln):
    S, C = x_ref.shape
    half = width // 2
    win = rows + 2 * halo
    pad_ref[0:halo, :] = jnp.zeros((halo, C), F32)
    pad_ref[halo + S:halo + S + halo, :] = jnp.zeros((halo, C), F32)
    pad_ref[halo:halo + S, :] = x_ref[...]

    def tile(t, carry):
        r0 = pl.multiple_of(t * rows, rows)
        for cb in range(C // 128):
            cs = slice(cb * 128, (cb + 1) * 128)
            window = pad_ref[pl.ds(r0, win), cs]
            acc = jnp.zeros((rows, 128), F32) + b_ref[:, cs]
            for r in range(8):
                taps = [w for w in range(width) if (halo + w - half) % 8 == r]
                if taps:
                    rolled = window if r == 0 else pltpu.roll(window, win - r, 0)
                    for w in taps:
                        a0 = halo + w - half - r
                        acc = acc + rolled[a0:a0 + rows] * w_ref[w:w + 1, cs]
            tmp_ref[:, cs] = acc
        acc = tmp_ref[...]
        if with_ln:
            acc = _layernorm(acc, g_ref[...], beta_ref[...])
        o_ref[pl.ds(r0, rows), :] = _silu(acc).astype(o_ref.dtype)
        return carry

    lax.fori_loop(0, S // rows, tile, 0)


def _dwconv(x2, S, col_block, w, b, g, beta, *, with_ln, out_dtype, ncb=1, rows=128):
    B = x2.shape[0] // S
    C = 512
    width = w.shape[0]
    halo = 16
    assert width // 2 <= halo
    kern = functools.partial(_dwconv_kernel, width=width, halo=halo, rows=rows, with_ln=with_ln)
    return pl.pallas_call(
        kern,
        grid=(B, ncb),
        in_specs=[
            pl.BlockSpec((S, C), lambda b, c: (b, col_block + c)),
            pl.BlockSpec((width, C), lambda b, c: (0, c)),
            pl.BlockSpec((1, C), lambda b, c: (0, c)),
            pl.BlockSpec((1, C), lambda b, c: (0, c)),
            pl.BlockSpec((1, C), lambda b, c: (0, c)),
        ],
        out_specs=pl.BlockSpec((S, C), lambda b, c: (b, c)),
        out_shape=jax.ShapeDtypeStruct((B * S, C * ncb), out_dtype),
        scratch_shapes=[pltpu.VMEM((S + 2 * halo, C), F32), pltpu.VMEM((rows, C), F32)],
        compiler_params=_cparams(("parallel", "parallel")),
        name="dwconv_ln" if with_ln else "dwconv",
    )(x2, w, b, g, beta)


ATT_HEADS_PER_STEP = 8


def _attn_kernel(q_ref, k_ref, v_ref, o_ref):
    for pair in range(ATT_HEADS_PER_STEP // 2):
        acc = None
        for j in range(2):
            hd = 2 * pair + j
            sl = slice(hd * HEAD_PAD, (hd + 1) * HEAD_PAD)
            s = _dot_nt(q_ref[:, sl], k_ref[:, sl])
            m = jnp.max(s, -1, keepdims=True)
            p = jnp.exp2(s - m)
            l = jnp.sum(p, -1, keepdims=True)
            o = _dot(p.astype(BF16), v_ref[:, sl]) / l
            acc = o if acc is None else acc + o
        o_ref[:, pair * 2 * MLA_V:(pair + 1) * 2 * MLA_V] = acc.astype(o_ref.dtype)


def _attention(q, k, v, B, S, tq=256):
    nq = S // tq
    T = B * S
    hps = ATT_HEADS_PER_STEP
    return pl.pallas_call(
        _attn_kernel,
        grid=(B, MLA_HEADS // hps, nq),
        in_specs=[
            pl.BlockSpec((tq, hps * HEAD_PAD), lambda b, hp, i: (b * nq + i, hp)),
            pl.BlockSpec((S, hps * HEAD_PAD), lambda b, hp, i: (b, hp), pipeline_mode=pl.Buffered(1)),
            pl.BlockSpec((S, hps * HEAD_PAD), lambda b, hp, i: (b, hp), pipeline_mode=pl.Buffered(1)),
        ],
        out_specs=pl.BlockSpec((tq, hps * MLA_V), lambda b, hp, i: (b * nq + i, hp)),
        out_shape=jax.ShapeDtypeStruct((T, MLA_HEADS * MLA_V), BF16),
        compiler_params=_cparams(("parallel", "parallel", "parallel")),
        name="attention",
    )(q, k, v)


def _ev_out_kernel(x_ref, u_ref, a_ref, w1_ref, w2_ref, g_ref, b_ref, o_ref):
    m = _dot(u_ref[...], w1_ref[...]) + _dot(a_ref[...], w2_ref[...])
    o_ref[...] = _layernorm(ALPHA * x_ref[...] + m, g_ref[...], b_ref[...])


def _ev_out(x2, u, att, w1, w2, g, b, tm=512):
    T = x2.shape[0]
    row = lambda i: (i, 0)
    return pl.pallas_call(
        _ev_out_kernel,
        grid=(T // tm,),
        in_specs=[
            pl.BlockSpec((tm, D_MODEL), row),
            pl.BlockSpec((tm, CONV_CH), row),
            pl.BlockSpec((tm, MLA_HEADS * MLA_V), row),
            _const_spec(w1.shape), _const_spec(w2.shape), _const_spec(g.shape), _const_spec(b.shape),
        ],
        out_specs=pl.BlockSpec((tm, D_MODEL), row),
        out_shape=jax.ShapeDtypeStruct((T, D_MODEL), F32),
        compiler_params=_cparams(("parallel",)),
        name="ev_out",
    )(x2, u, att, w1, w2, g, b)


def _ffn_kernel(x_ref, wg_ref, wu_ref, wd_ref, g_ref, b_ref, o_ref, *, fc):
    x = x_ref[...]
    xb = x.astype(BF16)
    acc = jnp.zeros(x.shape, F32)
    for c in range(wg_ref.shape[1] // fc):
        sl = slice(c * fc, (c + 1) * fc)
        hh = _silu(_dot(xb, wg_ref[:, sl])) * _dot(xb, wu_ref[:, sl])
        acc = acc + _dot(hh.astype(BF16), wd_ref[sl, :])
    o_ref[...] = _layernorm(ALPHA * x + acc, g_ref[...], b_ref[...])


def _ffn(x2, wg, wu, wd, g, b, tm=512, fc=256):
    T = x2.shape[0]
    row = lambda i: (i, 0)
    return pl.pallas_call(
        functools.partial(_ffn_kernel, fc=fc),
        grid=(T // tm,),
        in_specs=[
            pl.BlockSpec((tm, D_MODEL), row),
            _const_spec(wg.shape), _const_spec(wu.shape), _const_spec(wd.shape),
            _const_spec(g.shape), _const_spec(b.shape),
        ],
        out_specs=pl.BlockSpec((tm, D_MODEL), row),
        out_shape=jax.ShapeDtypeStruct((T, D_MODEL), F32),
        compiler_params=_cparams(("parallel",)),
        name="ffn",
    )(x2, wg, wu, wd, g, b)


def _od_in_kernel(x_ref, w_ref, wkt_ref, o_ref, kt_ref):
    xb = x_ref[...].astype(BF16)
    o_ref[...] = _dot(xb, w_ref[...])
    kt_ref[...] = _dot_nt(wkt_ref[...], xb) * (ML_HEAD_DIM ** -0.5)


def _od_in(x2, w, wkt, seq, tm=512):
    T = x2.shape[0]
    nps = seq // tm
    row = lambda i: (i, 0)
    return pl.pallas_call(
        _od_in_kernel,
        grid=(T // tm,),
        in_specs=[pl.BlockSpec((tm, D_MODEL), row), _const_spec(w.shape), _const_spec(wkt.shape)],
        out_specs=[pl.BlockSpec((tm, OD_COLS), row),
                   pl.BlockSpec((ML_INNER, tm), lambda i: (i // nps, i % nps))],
        out_shape=[jax.ShapeDtypeStruct((T, OD_COLS), F32),
                   jax.ShapeDtypeStruct((T // seq * ML_INNER, seq), F32)],
        compiler_params=_cparams(("parallel",)),
        name="od_in",
    )(x2, w, wkt)


def _tri(reverse):
    i = lax.broadcasted_iota(jnp.int32, (CHUNK, CHUNK), 0)
    j = lax.broadcasted_iota(jnp.int32, (CHUNK, CHUNK), 1)
    mask = (j >= i) if reverse else (j <= i)
    return mask, mask.astype(F32)


def _softplus(v):
    return jnp.maximum(v, 0.0) + jnp.log1p(jnp.exp(-jnp.abs(v)))


def _ssd_body(xbc_ref, sm_ref, bias_ref, a_ref, y_ref, st_ref, *, reverse, lane0):
    mask, tri = _tri(reverse)
    last = 0 if reverse else CHUNK - 1
    dt_all = _softplus(sm_ref[...] + bias_ref[...])
    da_all = dt_all * a_ref[...]
    cs_all = jnp.dot(tri, da_all, preferred_element_type=F32, precision=lax.Precision.HIGHEST)
    cs_t = cs_all.T
    tot_all = cs_all[last:last + 1, :]
    grow_all = jnp.exp(cs_all)
    rest_all = jnp.exp(tot_all - cs_all)
    etot_all = jnp.exp(tot_all)
    P = SSD_HEAD_DIM
    lo = lax.broadcasted_iota(jnp.int32, (CHUNK, 2 * P), 1) < P
    lo_row = lo[0:1, :]
    pairs_per_group = SSD_HEADS // SSD_GROUPS // 2
    prev = [st_ref[p] for p in range(SSD_HEADS // 2)]

    def pick(arr, l0):
        return jnp.where(lo if arr.shape[0] > 1 else lo_row, arr[:, l0:l0 + 1], arr[:, l0 + 1:l0 + 2])

    for g in range(SSD_GROUPS):
        b0 = SSD_INNER + g * SSD_STATE
        c0 = SSD_INNER + SSD_GROUPS * SSD_STATE + g * SSD_STATE
        cm = xbc_ref[:, c0:c0 + SSD_STATE].astype(BF16)
        bm_t = xbc_ref[:, b0:b0 + SSD_STATE].T.astype(BF16)
        cb = _dot(cm, bm_t)
        for pp in range(pairs_per_group):
            p = g * pairs_per_group + pp
            l0 = lane0 + 2 * p
            ps = slice(p * 2 * P, (p + 1) * 2 * P)
            xdt = xbc_ref[:, ps] * pick(dt_all, l0)
            x_lo = jnp.where(lo, xdt, 0.0).astype(BF16)
            x_hi = jnp.where(lo, 0.0, xdt).astype(BF16)
            dec0 = jnp.exp(jnp.where(mask, cs_all[:, l0:l0 + 1] - cs_t[l0:l0 + 1, :], -jnp.inf))
            dec1 = jnp.exp(jnp.where(mask, cs_all[:, l0 + 1:l0 + 2] - cs_t[l0 + 1:l0 + 2, :], -jnp.inf))
            y_diag = _dot((cb * dec0).astype(BF16), x_lo) + _dot((cb * dec1).astype(BF16), x_hi)
            y_off = _dot(cm, prev[p].astype(BF16)) * pick(grow_all, l0)
            y_ref[:, ps] = y_diag + y_off
            xw = (xdt * pick(rest_all, l0)).astype(BF16)
            st_ref[p] = pick(etot_all, l0) * prev[p] + _dot(bm_t, xw)


def _running_max(x, reverse):
    n = x.shape[0]
    row = lax.broadcasted_iota(jnp.int32, x.shape, 0)
    s = 1
    while s < n:
        if reverse:
            sh = jnp.where(row < n - s, pltpu.roll(x, n - s, 0), -jnp.inf)
        else:
            sh = jnp.where(row >= s, pltpu.roll(x, s, 0), -jnp.inf)
        x = jnp.maximum(x, sh)
        s *= 2
    return x


def _mlstm_body(q_ref, kt_ref, v_ref, ig_ref, fg_ref, ib_ref, fb_ref, h_ref, cn_ref, m_ref, *, reverse, direction):
    L = CHUNK
    mask, tri = _tri(reverse)
    last = 0 if reverse else L - 1
    li = ig_ref[...] + ib_ref[...]
    pre = fg_ref[...] + fb_ref[...]
    lf = jnp.minimum(pre, 0.0) - jnp.log1p(jnp.exp(-jnp.abs(pre)))
    bc = jnp.dot(tri, lf, preferred_element_type=F32, precision=lax.Precision.HIGHEST)
    u = li - bc
    cm = _running_max(u, reverse)
    m_prev = m_ref[...]
    big_m = jnp.maximum(m_prev, cm)
    g = bc[last:last + 1, :]
    m_loc = g + cm[last:last + 1, :]
    u_t = u.T
    e_end_t = jnp.exp(g + u - m_loc).T
    m_new = jnp.maximum(g + m_prev, m_loc)
    a_old = jnp.exp(g + m_prev - m_new)
    a_new = jnp.exp(m_loc - m_new)
    w_inter = jnp.exp(m_prev - big_m)
    emt = jnp.exp(-(bc + big_m))
    m_ref[...] = m_new

    lane = lax.broadcasted_iota(jnp.int32, (L, 128), 1)
    lo = lane < ML_HEAD_DIM
    row = lax.broadcasted_iota(jnp.int32, (128, 2 * 128), 0)
    col = lax.broadcasted_iota(jnp.int32, (128, 2 * 128), 1)
    block_diag = (row < ML_HEAD_DIM) == ((col % 128) < ML_HEAD_DIM)
    row_lo = lax.broadcasted_iota(jnp.int32, (128, L), 0) < ML_HEAD_DIM
    ones = jnp.ones((L, 128), F32)
    npairs = ML_HEADS // 2
    prev = [cn_ref[p] for p in range(npairs)]
    for p in range(npairs):
        l0 = direction * ML_HEADS + 2 * p
        l1 = l0 + 1
        ps = slice(p * 128, (p + 1) * 128)
        qp = q_ref[:, ps]
        vp = v_ref[:, ps]
        kt = kt_ref[ps, :]
        ktb = kt.astype(BF16)
        q_lo = jnp.where(lo, qp, 0.0).astype(BF16)
        q_hi = jnp.where(lo, 0.0, qp).astype(BF16)
        vo_lo = jnp.concatenate([jnp.where(lo, vp, 0.0), jnp.where(lo, ones, 0.0)], -1).astype(BF16)
        vo_hi = jnp.concatenate([jnp.where(lo, 0.0, vp), jnp.where(lo, 0.0, ones)], -1).astype(BF16)
        w0 = jnp.where(mask, jnp.exp(u_t[l0:l0 + 1, :] - big_m[:, l0:l0 + 1]), 0.0)
        w1 = jnp.where(mask, jnp.exp(u_t[l1:l1 + 1, :] - big_m[:, l1:l1 + 1]), 0.0)
        a0 = (_dot(q_lo, ktb) * w0).astype(BF16)
        a1 = (_dot(q_hi, ktb) * w1).astype(BF16)
        wi = jnp.where(lo, w_inter[:, l0:l0 + 1], w_inter[:, l1:l1 + 1])
        wi2 = jnp.concatenate([wi, wi], -1)
        nd = _dot(a0, vo_lo) + _dot(a1, vo_hi) + wi2 * _dot(qp.astype(BF16), prev[p].astype(BF16))
        floor = jnp.where(lo, emt[:, l0:l0 + 1], emt[:, l1:l1 + 1])
        h_ref[:, ps] = nd[:, :128] / jnp.maximum(jnp.abs(nd[:, 128:]), floor)
        e_t = jnp.where(row_lo, e_end_t[l0:l0 + 1, :], e_end_t[l1:l1 + 1, :])
        kte = (kt * e_t).astype(BF16)
        vo = jnp.concatenate([vp, ones], -1).astype(BF16)
        s_loc = jnp.where(block_diag, _dot(kte, vo), 0.0)
        row2 = lax.broadcasted_iota(jnp.int32, (128, 1), 0) < ML_HEAD_DIM
        ao = jnp.where(row2, a_old[:, l0:l0 + 1], a_old[:, l1:l1 + 1])
        an = jnp.where(row2, a_new[:, l0:l0 + 1], a_new[:, l1:l1 + 1])
        cn_ref[p] = ao * prev[p] + an * s_loc


def _odd_mix_kernel(xbc_f, dt_f, q_f, kt_f, v_f, ig_f, fg_f, xbc_r, dt_r, q_r, kt_r, v_r, ig_r, fg_r,
                    dtb_f, a_f, dtb_r, a_r, ib_ref, fb_ref, yf_ref, yb_ref, hf_ref, hb_ref,
                    st_f, st_r, cn_f, cn_r, m_f, m_r):
    @pl.when(pl.program_id(1) == 0)
    def _():
        for ref in (st_f, st_r, cn_f, cn_r, m_f, m_r):
            ref[...] = jnp.zeros(ref.shape, F32)

    _ssd_body(xbc_f, dt_f, dtb_f, a_f, yf_ref, st_f, reverse=False, lane0=0)
    _ssd_body(xbc_r, dt_r, dtb_r, a_r, yb_ref, st_r, reverse=True, lane0=SSD_HEADS)
    _mlstm_body(q_f, kt_f, v_f, ig_f, fg_f, ib_ref, fb_ref, hf_ref, cn_f, m_f, reverse=False, direction=0)
    _mlstm_body(q_r, kt_r, v_r, ig_r, fg_r, ib_ref, fb_ref, hb_ref, cn_r, m_r, reverse=True, direction=1)


def _odd_mix(xbc, proj, kt, S, w):
    T = proj.shape[0]
    B = T // S
    nc = S // CHUNK
    fwd = lambda c: c
    rev = lambda c: nc - 1 - c

    def specs(cidx):
        col = lambda cb: (lambda b, c: (b * nc + cidx(c), cb))
        return [
            pl.BlockSpec((CHUNK, SSD_XBC), col(0)),
            pl.BlockSpec((CHUNK, 128), col(OD_DT // 128)),
            pl.BlockSpec((CHUNK, ML_INNER), col(OD_Q // 512)),
            pl.BlockSpec((ML_INNER, CHUNK), lambda b, c: (b, cidx(c))),
            pl.BlockSpec((CHUNK, ML_INNER), col(OD_V // 512)),
            pl.BlockSpec((CHUNK, 128), col(OD_IG // 128)),
            pl.BlockSpec((CHUNK, 128), col(OD_FG // 128)),
        ]

    rows = [w["dt_bias"][0], w["a"][0], w["dt_bias"][1], w["a"][1], w["ig_b"], w["fg_b"]]
    out = lambda cidx: pl.BlockSpec((CHUNK, 512), lambda b, c: (b * nc + cidx(c), 0))
    seq = (xbc, proj, proj, kt, proj, proj, proj)
    return pl.pallas_call(
        _odd_mix_kernel,
        grid=(B, nc),
        in_specs=specs(fwd) + specs(rev) + [_const_spec(r.shape) for r in rows],
        out_specs=[out(fwd), out(rev), out(fwd), out(rev)],
        out_shape=[jax.ShapeDtypeStruct((T, 512), F32)] * 4,
        scratch_shapes=[
            pltpu.VMEM((SSD_HEADS // 2, SSD_STATE, 2 * SSD_HEAD_DIM), F32),
            pltpu.VMEM((SSD_HEADS // 2, SSD_STATE, 2 * SSD_HEAD_DIM), F32),
            pltpu.VMEM((ML_HEADS // 2, 128, 256), F32),
            pltpu.VMEM((ML_HEADS // 2, 128, 256), F32),
            pltpu.VMEM((1, 128), F32),
            pltpu.VMEM((1, 128), F32),
        ],
        compiler_params=_cparams(("parallel", "arbitrary")),
        name="odd_mix",
    )(*seq, *seq, *rows)


def _od_out_kernel(x_ref, z_ref, xs_ref, o_ref, yf_ref, yb_ref, hf_ref, hb_ref, dsk_ref, sg_ref, mg_ref, avg_ref,
                   w1_ref, w2_ref, g_ref, b_ref, rw_ref, rb_ref, x1_ref, route_ref):
    y = (yf_ref[...] + yb_ref[...] + xs_ref[...] * dsk_ref[...]) * _silu(z_ref[...])
    gw = SSD_INNER // SSD_GROUPS
    m = None
    for g in range(SSD_GROUPS):
        sl = slice(g * gw, (g + 1) * gw)
        yn = _rmsnorm(y[:, sl], sg_ref[:, sl]).astype(BF16)
        t = _dot(yn, w1_ref[sl, :])
        m = t if m is None else m + t
    hs = hf_ref[...] + hb_ref[...]
    avg = avg_ref[...]

    def head_mean(v):
        hi = v.astype(BF16)
        lo = (v - hi.astype(F32)).astype(BF16)
        return _dot(hi, avg) + _dot(lo, avg)

    dv = hs - head_mean(hs)
    var = head_mean(dv * dv)
    hn = jax.nn.sigmoid(o_ref[...]) * (dv * lax.rsqrt(var + LN_EPS) * mg_ref[...])
    m = m + _dot(hn.astype(BF16), w2_ref[...])
    x1 = _layernorm(ALPHA * x_ref[...] + m, g_ref[...], b_ref[...])
    x1_ref[...] = x1
    logits = _dot(x1.astype(BF16), rw_ref[...]) + rb_ref[...]
    lane = lax.broadcasted_iota(jnp.int32, logits.shape, 1).astype(F32)
    m1 = jnp.max(logits, -1, keepdims=True)
    i1 = jnp.min(jnp.where(logits == m1, lane, 128.0), -1, keepdims=True)
    rest = jnp.where(lane == i1, -jnp.inf, logits)
    m2 = jnp.max(rest, -1, keepdims=True)
    i2 = jnp.min(jnp.where(rest == m2, lane, 128.0), -1, keepdims=True)
    e = jnp.exp(m2 - m1)
    g1 = 1.0 / (1.0 + e)
    g2 = e / (1.0 + e)
    route_ref[...] = jnp.where(lane == 0.0, i1,
                               jnp.where(lane == 1.0, i2, jnp.where(lane == 2.0, g1, jnp.where(lane == 3.0, g2, 0.0))))


def _od_out(x2, proj, xbc, yf, yb, hf, hb, dsk, sg, mg, w1, w2, g, b, rw, rb, tm=256):
    T = x2.shape[0]
    row = lambda i: (i, 0)
    colb = lambda cb: (lambda i: (i, cb))
    head = jnp.arange(ML_INNER) // ML_HEAD_DIM
    avg = jnp.where(head[:, None] == head[None, :], 1.0 / ML_HEAD_DIM, 0.0).astype(BF16)
    return pl.pallas_call(
        _od_out_kernel,
        grid=(T // tm,),
        in_specs=[
            pl.BlockSpec((tm, D_MODEL), row),
            pl.BlockSpec((tm, 512), colb(OD_Z // 512)),
            pl.BlockSpec((tm, 512), colb(0)),
            pl.BlockSpec((tm, 512), colb(OD_O // 512)),
            pl.BlockSpec((tm, 512), row), pl.BlockSpec((tm, 512), row),
            pl.BlockSpec((tm, 512), row), pl.BlockSpec((tm, 512), row),
            _const_spec(dsk.shape), _const_spec(sg.shape), _const_spec(mg.shape), _const_spec(avg.shape),
            _const_spec(w1.shape), _const_spec(w2.shape), _const_spec(g.shape), _const_spec(b.shape),
            _const_spec(rw.shape), _const_spec(rb.shape),
        ],
        out_specs=[pl.BlockSpec((tm, D_MODEL), row), pl.BlockSpec((tm, 128), row)],
        out_shape=[jax.ShapeDtypeStruct((T, D_MODEL), F32), jax.ShapeDtypeStruct((T, 128), F32)],
        compiler_params=_cparams(("parallel",)),
        name="od_out",
    )(x2, proj, xbc, proj, yf, yb, hf, hb, dsk, sg, mg, avg, w1, w2, g, b, rw, rb)


def _gather_rows(src_hbm, idx_smem, dst, sem, n):
    for r in range(n):
        pltpu.make_async_copy(src_hbm.at[pl.ds(idx_smem[0, r], 1)], dst.at[pl.ds(r, 1)], sem).start(priority=r % 2)


def _gather_wait(src_hbm, dst, sem):
    pltpu.make_async_copy(src_hbm.at[pl.ds(0, dst.shape[0])], dst, sem).wait()


def _gather_loop(src_hbm, idx_smem, idx0, dst, sem):
    def start(r, carry):
        pltpu.make_async_copy(src_hbm.at[pl.ds(idx_smem[0, idx0 + r], 1)], dst.at[pl.ds(r, 1)], sem).start()
        return carry

    lax.fori_loop(0, dst.shape[0], start, 0, unroll=8)


def _moe_kernel(te_ref, tok0_ref, tokn_ref, gate_ref, x_hbm, wg_ref, wu_ref, wd_ref, o_ref,
                xbuf, xb_ref, acc_ref, sem, *, nfc, sub):
    i = pl.program_id(0)
    j = pl.program_id(1)
    n = pl.num_programs(0)
    per_step = xbuf.shape[1]

    @pl.when(jnp.logical_and(i == 0, j == 0))
    def _():
        for jj in range(nfc):
            _gather_loop(x_hbm, tok0_ref, jj * per_step, xbuf.at[jj], sem)

    @pl.when(j == 0)
    def _():
        for jj in range(nfc):
            _gather_wait(x_hbm, xbuf.at[jj], sem)
        for jj in range(nfc):
            xb_ref[jj * per_step:(jj + 1) * per_step, :] = xbuf[jj].astype(BF16)
        acc_ref[...] = jnp.zeros(acc_ref.shape, F32)

    xb = xb_ref[...]
    acc = acc_ref[...]
    for c in range(wg_ref.shape[1] // sub):
        sl = slice(c * sub, (c + 1) * sub)
        hh = _silu(_dot(xb, wg_ref[:, sl])) * _dot(xb, wu_ref[:, sl])
        acc = acc + _dot(hh.astype(BF16), wd_ref[sl, :])
    acc_ref[...] = acc
    _gather_rows(x_hbm, tokn_ref, xbuf.at[j], sem, per_step)

    @pl.when(j == nfc - 1)
    def _():
        o_ref[...] = acc_ref[...] * gate_ref[...]

    @pl.when(jnp.logical_and(i == n - 1, j == nfc - 1))
    def _():
        for jj in range(nfc):
            _gather_wait(x_hbm, xbuf.at[jj], sem)


def _moe(x1, tile_expert, row_token, row_gate, wg, wu, wd, tm, fc=1792, sub=256):
    n_tiles = tile_expert.shape[0]
    nfc = D_FF_EXPERT // fc
    per_step = tm // nfc
    grid_spec = pltpu.PrefetchScalarGridSpec(
        num_scalar_prefetch=1,
        grid=(n_tiles, nfc),
        in_specs=[
            pl.BlockSpec((None, 1, tm), lambda i, j, te: (0, 0, 0), memory_space=pltpu.SMEM),
            pl.BlockSpec((None, 1, per_step), lambda i, j, te: ((i + 1) * nfc + j, 0, 0), memory_space=pltpu.SMEM),
            pl.BlockSpec((tm, 1), lambda i, j, te: (i, 0)),
            pl.BlockSpec(memory_space=pl.ANY),
            pl.BlockSpec((None, D_MODEL, fc), lambda i, j, te: (te[i], 0, j)),
            pl.BlockSpec((None, D_MODEL, fc), lambda i, j, te: (te[i], 0, j)),
            pl.BlockSpec((None, fc, D_MODEL), lambda i, j, te: (te[i], j, 0)),
        ],
        out_specs=pl.BlockSpec((tm, D_MODEL), lambda i, j, te: (i, 0)),
        scratch_shapes=[
            pltpu.VMEM((nfc, per_step, D_MODEL), F32),
            pltpu.VMEM((tm, D_MODEL), BF16),
            pltpu.VMEM((tm, D_MODEL), F32),
            pltpu.SemaphoreType.DMA,
        ],
    )
    return pl.pallas_call(
        functools.partial(_moe_kernel, nfc=nfc, sub=sub),
        grid_spec=grid_spec,
        out_shape=jax.ShapeDtypeStruct((n_tiles * tm, D_MODEL), F32),
        compiler_params=_cparams(("arbitrary", "arbitrary")),
        name="moe_experts",
    )(tile_expert, row_token.reshape(n_tiles + 1, 1, tm), row_token.reshape((n_tiles + 1) * nfc, 1, per_step),
      row_gate, x1, wg, wu, wd)


def _combine_kernel(pos0_ref, posa_ref, posb_ref, x_ref, y_hbm, g_ref, b_ref, o_ref, ybuf, sem, *, tm):
    s = pl.program_id(0)
    n = pl.num_programs(0)

    @pl.when(s == 0)
    def _():
        for half in range(2):
            _gather_loop(y_hbm, pos0_ref, half * 2 * tm, ybuf.at[half], sem.at[half])

    for half, pos_ref in ((0, posa_ref), (1, posb_ref)):
        _gather_wait(y_hbm, ybuf.at[half], sem.at[half])
        f = ybuf[half, 0:tm, :] + ybuf[half, tm:2 * tm, :]
        _gather_rows(y_hbm, pos_ref, ybuf.at[half], sem.at[half], 2 * tm)
        rows = slice(half * tm, (half + 1) * tm)
        o_ref[rows, :] = _layernorm(ALPHA * x_ref[rows, :] + f, g_ref[...], b_ref[...])

    @pl.when(s == n - 1)
    def _():
        _gather_wait(y_hbm, ybuf.at[0], sem.at[0])
        _gather_wait(y_hbm, ybuf.at[1], sem.at[1])


def _combine(x1, y_sorted, pos, g, b, tm=256):
    T = x1.shape[0]
    nt = T // tm
    assert nt % 2 == 0
    return pl.pallas_call(
        functools.partial(_combine_kernel, tm=tm),
        grid=(nt // 2,),
        in_specs=[
            pl.BlockSpec((None, 1, 4 * tm), lambda s: (0, 0, 0), memory_space=pltpu.SMEM),
            pl.BlockSpec((None, 1, 2 * tm), lambda s: (2 * s + 2, 0, 0), memory_space=pltpu.SMEM),
            pl.BlockSpec((None, 1, 2 * tm), lambda s: (2 * s + 3, 0, 0), memory_space=pltpu.SMEM),
            pl.BlockSpec((2 * tm, D_MODEL), lambda s: (s, 0)),
            pl.BlockSpec(memory_space=pl.ANY),
            _const_spec(g.shape), _const_spec(b.shape),
        ],
        out_specs=pl.BlockSpec((2 * tm, D_MODEL), lambda s: (s, 0)),
        out_shape=jax.ShapeDtypeStruct((T, D_MODEL), F32),
        scratch_shapes=[pltpu.VMEM((2, 2 * tm, D_MODEL), F32), pltpu.SemaphoreType.DMA((2,))],
        compiler_params=_cparams(("arbitrary",)),
        name="moe_combine",
    )(pos.reshape((nt + 2) // 2, 1, 4 * tm), pos, pos, x1, y_sorted, g, b)


def _route_tables(route, tm_e, tm_c):
    T = route.shape[0]
    e_flat = route[:, 0:2].astype(jnp.int32).reshape(-1)
    g_flat = route[:, 2:4].reshape(-1)
    A = 2 * T
    order = jnp.argsort(e_flat, stable=True).astype(jnp.int32)
    onehot = jax.nn.one_hot(e_flat, N_EXPERTS, dtype=jnp.int32)
    before = jnp.cumsum(onehot, axis=0) - onehot
    counts = jnp.sum(onehot, axis=0)
    start = jnp.cumsum(counts) - counts
    rank = start[e_flat] + jnp.sum(before * onehot, axis=1)
    padded = ((counts + tm_e - 1) // tm_e) * tm_e
    pend = jnp.cumsum(padded)
    pstart = pend - padded
    n_tiles = A // tm_e + N_EXPERTS
    rows = jnp.arange(n_tiles * tm_e, dtype=jnp.int32)
    e_row = jnp.minimum(jnp.searchsorted(pend, rows, side="right"), N_EXPERTS - 1).astype(jnp.int32)
    local = rows - pstart[e_row]
    ok = jnp.logical_and(local < counts[e_row], rows < pend[-1])
    a_row = order[jnp.clip(start[e_row] + local, 0, A - 1)]
    row_token = jnp.where(ok, a_row // 2, 0).astype(jnp.int32)
    row_token = jnp.concatenate([row_token, jnp.zeros((tm_e,), jnp.int32)])
    row_gate = jnp.where(ok, g_flat[a_row], 0.0).astype(F32).reshape(-1, 1)
    tile_rows = jnp.arange(n_tiles, dtype=jnp.int32) * tm_e
    tile_expert = e_row[tile_rows]
    dest = (pstart[e_flat] + rank - start[e_flat]).reshape(T, 2)
    pos = dest.reshape(T // tm_c, tm_c, 2).transpose(0, 2, 1).reshape(T // tm_c, 1, 2 * tm_c)
    pos = jnp.concatenate([pos, jnp.zeros((2, 1, 2 * tm_c), jnp.int32)], 0).astype(jnp.int32)
    return tile_expert, row_token, row_gate, pos


def _rot_cols(w):
    half = MLA_ROPE // 2
    return jnp.concatenate([-w[..., half:], w[..., :half]], -1)


def _prep_even(p, j):
    w_in = p["ev_w_in"][j]
    c_rot = 2 * CONV_CH + MLA_Q_LORA + MLA_KV_LORA
    k_rot = w_in[:, c_rot:c_rot + MLA_ROPE]
    z64 = jnp.zeros((D_MODEL, MLA_NOPE), F32)
    z32 = jnp.zeros((D_MODEL, HEAD_PAD - MLA_NOPE - MLA_ROPE), F32)
    w_in2 = jnp.concatenate([w_in[:, :c_rot], z64, k_rot, z32, z64, _rot_cols(k_rot), z32], -1).astype(BF16)
    wq = p["mla_w_uq"][j].reshape(MLA_Q_LORA, MLA_HEADS, MLA_NOPE + MLA_ROPE)
    zq = jnp.zeros((MLA_Q_LORA, MLA_HEADS, HEAD_PAD - MLA_NOPE - MLA_ROPE), F32)
    zq64 = jnp.zeros((MLA_Q_LORA, MLA_HEADS, MLA_NOPE), F32)
    wq_plain = jnp.concatenate([wq, zq], -1).reshape(MLA_Q_LORA, -1)
    wq_rot = jnp.concatenate([zq64, _rot_cols(wq[..., MLA_NOPE:]), zq], -1).reshape(MLA_Q_LORA, -1)
    wq2 = jnp.concatenate([wq_plain, wq_rot], -1).astype(BF16)
    wkv = p["mla_w_ukv"][j].reshape(MLA_KV_LORA, MLA_HEADS, MLA_NOPE + MLA_V)
    zk = jnp.zeros((MLA_KV_LORA, MLA_HEADS, MLA_V), F32)
    wk = jnp.concatenate([wkv[..., :MLA_NOPE], zk], -1).reshape(MLA_KV_LORA, -1)
    wv = wkv[..., MLA_NOPE:]
    even = (jnp.arange(MLA_HEADS) % 2 == 0)[None, :, None]
    wv2 = jnp.concatenate([jnp.where(even, wv, 0.0), jnp.where(even, 0.0, wv)], -1).reshape(MLA_KV_LORA, -1)
    wkv2 = jnp.concatenate([wk, wv2], -1).astype(BF16)
    w_out = p["ev_w_out"][j].astype(BF16)
    return dict(
        w_in=w_in2, qg=p["mla_q_norm_g"][j][None], wq=wq2, kvg=p["mla_kv_norm_g"][j][None], wkv=wkv2,
        dw_w=p["conv_dw_w"][j], dw_b=p["conv_dw_b"][j][None], cln_g=p["conv_ln_g"][j][None],
        cln_b=p["conv_ln_b"][j][None], w1=w_out[:CONV_CH], w2=w_out[CONV_CH:],
        wg=p["ffn_w_gate"][j].astype(BF16), wu=p["ffn_w_up"][j].astype(BF16), wd=p["ffn_w_down"][j].astype(BF16),
    )


def _lane_row(vals, lane0):
    return jnp.zeros((1, 128), F32).at[0, lane0:lane0 + vals.shape[0]].set(vals)


def _prep_odd(p, j):
    w = p["od_w_in"][j]
    z112 = jnp.zeros((D_MODEL, 128 - 16), F32)
    w_in2 = jnp.concatenate([w[:, 0:1536], w[:, 1552:2064], w[:, 2576:3600],
                             w[:, 1536:1552], z112, w[:, 3600:3616], z112, w[:, 3616:3632], z112], -1).astype(BF16)
    wkt = w[:, 2064:2576].T.astype(BF16)
    w_out = p["od_w_out"][j].astype(BF16)
    rw = jnp.concatenate([p["moe_router_w"][j], jnp.zeros((D_MODEL, 128 - N_EXPERTS), F32)], -1).astype(BF16)
    rb = jnp.full((1, 128), -jnp.inf, F32).at[0, :N_EXPERTS].set(p["moe_router_b"][j])
    a = -jnp.exp(p["ssd_a_log"][j])
    return dict(
        w_in=w_in2, wkt=wkt, cw=p["ssd_conv_w"][j], cb=p["ssd_conv_b"][j][None],
        dt_bias=[_lane_row(p["ssd_dt_bias"][j][d], d * SSD_HEADS) for d in range(2)],
        a=[_lane_row(a[d], d * SSD_HEADS) for d in range(2)],
        ig_b=_lane_row(p["ml_igate_b"][j].reshape(-1), 0), fg_b=_lane_row(p["ml_fgate_b"][j].reshape(-1), 0),
        dsk=jnp.repeat(p["ssd_d"][j], SSD_HEAD_DIM)[None], sg=p["ssd_norm_g"][j][None], mg=p["ml_norm_g"][j][None],
        w1=w_out[:SSD_INNER], w2=w_out[SSD_INNER:], rw=rw, rb=rb,
        wg=p["moe_w_gate"][j].astype(BF16), wu=p["moe_w_up"][j].astype(BF16), wd=p["moe_w_down"][j].astype(BF16),
    )


def _rope_tables(seq):
    half = MLA_ROPE // 2
    inv_freq = ROPE_THETA ** (-jnp.arange(half, dtype=F32) / half)
    ang = jnp.arange(seq, dtype=F32)[:, None] * inv_freq
    cos2 = jnp.concatenate([jnp.cos(ang), jnp.cos(ang)], -1)
    sin2 = jnp.concatenate([jnp.sin(ang), jnp.sin(ang)], -1)
    pad = jnp.zeros((seq, HEAD_PAD - MLA_NOPE - MLA_ROPE), F32)
    cos_t = jnp.concatenate([jnp.ones((seq, MLA_NOPE), F32), cos2, pad], -1)
    sin_t = jnp.concatenate([jnp.zeros((seq, MLA_NOPE), F32), sin2, pad], -1)
    return cos_t, sin_t


def _even_layer(x2, B, S, w, ln, cos_t, sin_t):
    tm = min(512, S)
    u, q, k, v = _ev_in(x2, cos_t, sin_t, w["w_in"], w["qg"], w["wq"], w["kvg"], w["wkv"], S, tm=tm)
    uc = _dwconv(u, S, 0, w["dw_w"], w["dw_b"], w["cln_g"], w["cln_b"], with_ln=True, out_dtype=BF16)
    att = _attention(q, k, v, B, S, tq=min(256, S))
    x1 = _ev_out(x2, uc, att, w["w1"], w["w2"], ln[0], ln[1], tm=tm)
    return _ffn(x1, w["wg"], w["wu"], w["wd"], ln[2], ln[3], tm=tm)


def _odd_layer(x2, B, S, w, ln, tm_e=512, tm_c=256):
    proj, kt = _od_in(x2, w["w_in"], w["wkt"], S, tm=min(512, S))
    zeros = jnp.zeros((1, SSD_XBC), F32)
    xbc = _dwconv(proj, S, OD_XBC // 512, w["cw"], w["cb"], zeros, zeros, with_ln=False, out_dtype=F32, ncb=2)
    yf, yb, hf, hb = _odd_mix(xbc, proj, kt, S, w)
    x1, route = _od_out(x2, proj, xbc, yf, yb, hf, hb, w["dsk"], w["sg"], w["mg"],
                        w["w1"], w["w2"], ln[0], ln[1], w["rw"], w["rb"])
    te, row_token, row_gate, pos = _route_tables(route, tm_e, tm_c)
    y_sorted = _moe(x1, te, row_token, row_gate, w["wg"], w["wu"], w["wd"], tm_e)
    return _combine(x1, y_sorted, pos, ln[2], ln[3], tm_c)


def _trunk(x, p):
    B, S, _ = x.shape
    x2 = x.reshape(B * S, D_MODEL)
    cos_t, sin_t = _rope_tables(S)
    for l in range(DEPTH):
        j = l // 2
        ln = (p["ln1_g"][l][None], p["ln1_b"][l][None], p["ln2_g"][l][None], p["ln2_b"][l][None])
        if l % 2 == 0:
            x2 = _even_layer(x2, B, S, _prep_even(p, j), ln, cos_t, sin_t)
        else:
            x2 = _odd_layer(x2, B, S, _prep_odd(p, j), ln)
    return x2.reshape(B, S, D_MODEL)


def kernel(x_prompt, x_sample, ev_w_in, conv_dw_w, conv_dw_b, conv_ln_g, conv_ln_b, mla_q_norm_g, mla_w_uq, mla_kv_norm_g, mla_w_ukv, ev_w_out, od_w_in, ssd_conv_w, ssd_conv_b, ssd_dt_bias, ssd_a_log, ssd_d, ssd_norm_g, ml_igate_b, ml_fgate_b, ml_norm_g, od_w_out, ffn_w_gate, ffn_w_up, ffn_w_down, moe_router_w, moe_router_b, moe_w_gate, moe_w_up, moe_w_down, ln1_g, ln1_b, ln2_g, ln2_b):
    p = dict(ev_w_in=ev_w_in, conv_dw_w=conv_dw_w, conv_dw_b=conv_dw_b, conv_ln_g=conv_ln_g, conv_ln_b=conv_ln_b,
             mla_q_norm_g=mla_q_norm_g, mla_w_uq=mla_w_uq, mla_kv_norm_g=mla_kv_norm_g, mla_w_ukv=mla_w_ukv,
             ev_w_out=ev_w_out, od_w_in=od_w_in, ssd_conv_w=ssd_conv_w, ssd_conv_b=ssd_conv_b,
             ssd_dt_bias=ssd_dt_bias, ssd_a_log=ssd_a_log, ssd_d=ssd_d, ssd_norm_g=ssd_norm_g,
             ml_igate_b=ml_igate_b, ml_fgate_b=ml_fgate_b, ml_norm_g=ml_norm_g, od_w_out=od_w_out,
             ffn_w_gate=ffn_w_gate, ffn_w_up=ffn_w_up, ffn_w_down=ffn_w_down, moe_router_w=moe_router_w,
             moe_router_b=moe_router_b, moe_w_gate=moe_w_gate, moe_w_up=moe_w_up, moe_w_down=moe_w_down,
             ln1_g=ln1_g, ln1_b=ln1_b, ln2_g=ln2_g, ln2_b=ln2_b)
    assert x_prompt.shape[1] == x_sample.shape[1]
    nb = x_prompt.shape[0]
    y = _trunk(jnp.concatenate([x_prompt, x_sample], 0), p)
    return (y[:nb], y[nb:])
```

```python
import functools
import math

import jax
import jax.numpy as jnp
import numpy as np
from jax import lax
from jax.experimental import pallas as pl
from jax.experimental.pallas import tpu as pltpu

F32 = jnp.float32
BF16 = jnp.bfloat16

D_MODEL = 1024
DEPTH = 4
ALPHA = (2.0 * DEPTH) ** 0.25
LN_EPS = 1e-5
RMS_EPS = 1e-6

CONV_CH = 512
CONV_W = 31
MLA_HEADS = 8
MLA_NOPE = 64
MLA_ROPE = 32
MLA_V = 64
MLA_Q_LORA = 256
MLA_KV_LORA = 128
ROPE_THETA = 10000.0
HEAD_PAD = 128
EV_COLS = 2 * CONV_CH + MLA_Q_LORA + MLA_KV_LORA + 2 * HEAD_PAD
Q_SCALE = (MLA_NOPE + MLA_ROPE) ** -0.5 * math.log2(math.e)

SSD_HEADS = 8
SSD_HEAD_DIM = 64
SSD_INNER = 512
SSD_GROUPS = 2
SSD_STATE = 128
SSD_CONV_W = 5
SSD_XBC = 1024
CHUNK = 128
ML_HEADS = 8
ML_HEAD_DIM = 64
ML_INNER = 512
OD_Z, OD_XBC, OD_Q, OD_V, OD_O, OD_DT, OD_IG, OD_FG = 0, 512, 1536, 2048, 2560, 3072, 3200, 3328
OD_COLS = 3456

D_FF = 2816
N_EXPERTS = 8
D_FF_EXPERT = 3584

VMEM_LIMIT = 56 * 1024 * 1024


def _cparams(sem):
    return pltpu.CompilerParams(dimension_semantics=sem, vmem_limit_bytes=VMEM_LIMIT)


def _const_spec(shape):
    nd = len(shape)
    return pl.BlockSpec(shape, lambda *_: (0,) * nd, pipeline_mode=pl.Buffered(1))


def _layernorm(v, g, b):
    mu = jnp.mean(v, -1, keepdims=True)
    d = v - mu
    var = jnp.mean(d * d, -1, keepdims=True)
    return d * lax.rsqrt(var + LN_EPS) * g + b


def _rmsnorm(v, g):
    return v * lax.rsqrt(jnp.mean(v * v, -1, keepdims=True) + RMS_EPS) * g


def _silu(v):
    return v * jax.nn.sigmoid(v)


def _dot(a, b):
    return jnp.dot(a, b, preferred_element_type=F32)


def _dot_nt(a, b):
    return lax.dot_general(a, b, (((1,), (1,)), ((), ())), preferred_element_type=F32)


def _ev_in_kernel(x_ref, cos_ref, sin_ref, w_in_ref, qg_ref, wq_ref, kvg_ref, wkv_ref,
                  u_ref, q_ref, k_ref, v_ref):
    xb = x_ref[...].astype(BF16)
    h = _dot(xb, w_in_ref[...])
    u_ref[...] = h[:, :CONV_CH] * jax.nn.sigmoid(h[:, CONV_CH:2 * CONV_CH])
    c0 = 2 * CONV_CH
    cos = cos_ref[...]
    sin = sin_ref[...]
    ql = _rmsnorm(h[:, c0:c0 + MLA_Q_LORA], qg_ref[...]).astype(BF16)
    qq = _dot(ql, wq_ref[...])
    c1 = c0 + MLA_Q_LORA
    kvl = _rmsnorm(h[:, c1:c1 + MLA_KV_LORA], kvg_ref[...]).astype(BF16)
    kk = _dot(kvl, wkv_ref[...])
    c2 = c1 + MLA_KV_LORA
    kpe = h[:, c2:c2 + HEAD_PAD] * cos + h[:, c2 + HEAD_PAD:c2 + 2 * HEAD_PAD] * sin
    nh = MLA_HEADS * HEAD_PAD
    for hd in range(MLA_HEADS):
        sl = slice(hd * HEAD_PAD, (hd + 1) * HEAD_PAD)
        sl2 = slice(nh + hd * HEAD_PAD, nh + (hd + 1) * HEAD_PAD)
        q_ref[:, sl] = ((qq[:, sl] * cos + qq[:, sl2] * sin) * Q_SCALE).astype(BF16)
        k_ref[:, sl] = (kk[:, sl] + kpe).astype(BF16)
    v_ref[...] = kk[:, nh:].astype(BF16)


def _ev_in(x2, cos_t, sin_t, w_in, qg, wq, kvg, wkv, seq, tm=512):
    T = x2.shape[0]
    nps = seq // tm
    row = lambda i: (i, 0)
    pos = lambda i: (i % nps, 0)
    nh = MLA_HEADS * HEAD_PAD
    return pl.pallas_call(
        _ev_in_kernel,
        grid=(T // tm,),
        in_specs=[
            pl.BlockSpec((tm, D_MODEL), row),
            pl.BlockSpec((tm, HEAD_PAD), pos),
            pl.BlockSpec((tm, HEAD_PAD), pos),
            _const_spec(w_in.shape), _const_spec(qg.shape), _const_spec(wq.shape),
            _const_spec(kvg.shape), _const_spec(wkv.shape),
        ],
        out_specs=[
            pl.BlockSpec((tm, CONV_CH), row),
            pl.BlockSpec((tm, nh), row),
            pl.BlockSpec((tm, nh), row),
            pl.BlockSpec((tm, nh), row),
        ],
        out_shape=[
            jax.ShapeDtypeStruct((T, CONV_CH), F32),
            jax.ShapeDtypeStruct((T, nh), BF16),
            jax.ShapeDtypeStruct((T, nh), BF16),
            jax.ShapeDtypeStruct((T, nh), BF16),
        ],
        compiler_params=_cparams(("parallel",)),
        name="ev_in",
    )(x2, cos_t, sin_t, w_in, qg, wq, kvg, wkv)


def _dwconv_kernel(x_ref, w_ref, b_ref, g_ref, beta_ref, o_ref, pad_ref, tmp_ref, *, width, halo, rows, with_ln):
    S, C = x_ref.shape
    half = width // 2
    win = rows + 2 * halo
    pad_ref[0:halo, :] = jnp.zeros((halo, C), F32)
    pad_ref[halo + S:halo + S + halo, :] = jnp.zeros((halo, C), F32)
    pad_ref[halo:halo + S, :] = x_ref[...]

    def tile(t, carry):
        r0 = pl.multiple_of(t * rows, rows)
        for cb in range(C // 128):
            cs = slice(cb * 128, (cb + 1) * 128)
            window = pad_ref[pl.ds(r0, win), cs]
            acc = jnp.zeros((rows, 128), F32) + b_ref[:, cs]
            for r in range(8):
                taps = [w for w in range(width) if (halo + w - half) % 8 == r]
                if taps:
                    rolled = window if r == 0 else pltpu.roll(window, win - r, 0)
                    for w in taps:
                        a0 = halo + w - half - r
                        acc = acc + rolled[a0:a0 + rows] * w_ref[w:w + 1, cs]
            tmp_ref[:, cs] = acc
        acc = tmp_ref[...]
        if with_ln:
            acc = _layernorm(acc, g_ref[...], beta_ref[...])
        o_ref[pl.ds(r0, rows), :] = _silu(acc).astype(o_ref.dtype)
        return carry

    lax.fori_loop(0, S // rows, tile, 0)


def _dwconv(x2, S, col_block, w, b, g, beta, *, with_ln, out_dtype, ncb=1, rows=128):
    B = x2.shape[0] // S
    C = 512
    width = w.shape[0]
    halo = 16
    assert width // 2 <= halo
    kern = functools.partial(_dwconv_kernel, width=width, halo=halo, rows=rows, with_ln=with_ln)
    return pl.pallas_call(
        kern,
        grid=(B, ncb),
        in_specs=[
            pl.BlockSpec((S, C), lambda b, c: (b, col_block + c)),
            pl.BlockSpec((width, C), lambda b, c: (0, c)),
            pl.BlockSpec((1, C), lambda b, c: (0, c)),
            pl.BlockSpec((1, C), lambda b, c: (0, c)),
            pl.BlockSpec((1, C), lambda b, c: (0, c)),
        ],
        out_specs=pl.BlockSpec((S, C), lambda b, c: (b, c)),
        out_shape=jax.ShapeDtypeStruct((B * S, C * ncb), out_dtype),
        scratch_shapes=[pltpu.VMEM((S + 2 * halo, C), F32), pltpu.VMEM((rows, C), F32)],
        compiler_params=_cparams(("parallel", "parallel")),
        name="dwconv_ln" if with_ln else "dwconv",
    )(x2, w, b, g, beta)


ATT_HEADS_PER_STEP = 8


def _attn_kernel(q_ref, k_ref, v_ref, o_ref):
    for pair in range(ATT_HEADS_PER_STEP // 2):
        acc = None
        for j in range(2):
            hd = 2 * pair + j
            sl = slice(hd * HEAD_PAD, (hd + 1) * HEAD_PAD)
            s = _dot_nt(q_ref[:, sl], k_ref[:, sl])
            m = jnp.max(s, -1, keepdims=True)
            p = jnp.exp2(s - m)
            l = jnp.sum(p, -1, keepdims=True)
            o = _dot(p.astype(BF16), v_ref[:, sl]) / l
            acc = o if acc is None else acc + o
        o_ref[:, pair * 2 * MLA_V:(pair + 1) * 2 * MLA_V] = acc.astype(o_ref.dtype)


def _attention(q, k, v, B, S, tq=256):
    nq = S // tq
    T = B * S
    hps = ATT_HEADS_PER_STEP
    return pl.pallas_call(
        _attn_kernel,
        grid=(B, MLA_HEADS // hps, nq),
        in_specs=[
            pl.BlockSpec((tq, hps * HEAD_PAD), lambda b, hp, i: (b * nq + i, hp)),
            pl.BlockSpec((S, hps * HEAD_PAD), lambda b, hp, i: (b, hp), pipeline_mode=pl.Buffered(1)),
            pl.BlockSpec((S, hps * HEAD_PAD), lambda b, hp, i: (b, hp), pipeline_mode=pl.Buffered(1)),
        ],
        out_specs=pl.BlockSpec((tq, hps * MLA_V), lambda b, hp, i: (b * nq + i, hp)),
        out_shape=jax.ShapeDtypeStruct((T, MLA_HEADS * MLA_V), BF16),
        compiler_params=_cparams(("parallel", "parallel", "parallel")),
        name="attention",
    )(q, k, v)


def _ev_out_kernel(x_ref, u_ref, a_ref, w1_ref, w2_ref, g_ref, b_ref, o_ref):
    m = _dot(u_ref[...], w1_ref[...]) + _dot(a_ref[...], w2_ref[...])
    o_ref[...] = _layernorm(ALPHA * x_ref[...] + m, g_ref[...], b_ref[...])


def _ev_out(x2, u, att, w1, w2, g, b, tm=512):
    T = x2.shape[0]
    row = lambda i: (i, 0)
    return pl.pallas_call(
        _ev_out_kernel,
        grid=(T // tm,),
        in_specs=[
            pl.BlockSpec((tm, D_MODEL), row),
            pl.BlockSpec((tm, CONV_CH), row),
            pl.BlockSpec((tm, MLA_HEADS * MLA_V), row),
            _const_spec(w1.shape), _const_spec(w2.shape), _const_spec(g.shape), _const_spec(b.shape),
        ],
        out_specs=pl.BlockSpec((tm, D_MODEL), row),
        out_shape=jax.ShapeDtypeStruct((T, D_MODEL), F32),
        compiler_params=_cparams(("parallel",)),
        name="ev_out",
    )(x2, u, att, w1, w2, g, b)


def _ffn_kernel(x_ref, wg_ref, wu_ref, wd_ref, g_ref, b_ref, o_ref, *, fc):
    x = x_ref[...]
    xb = x.astype(BF16)
    acc = jnp.zeros(x.shape, F32)
    for c in range(wg_ref.shape[1] // fc):
        sl = slice(c * fc, (c + 1) * fc)
        hh = _silu(_dot(xb, wg_ref[:, sl])) * _dot(xb, wu_ref[:, sl])
        acc = acc + _dot(hh.astype(BF16), wd_ref[sl, :])
    o_ref[...] = _layernorm(ALPHA * x + acc, g_ref[...], b_ref[...])


def _ffn(x2, wg, wu, wd, g, b, tm=512, fc=256):
    T = x2.shape[0]
    row = lambda i: (i, 0)
    return pl.pallas_call(
        functools.partial(_ffn_kernel, fc=fc),
        grid=(T // tm,),
        in_specs=[
            pl.BlockSpec((tm, D_MODEL), row),
            _const_spec(wg.shape), _const_spec(wu.shape), _const_spec(wd.shape),
            _const_spec(g.shape), _const_spec(b.shape),
        ],
        out_specs=pl.BlockSpec((tm, D_MODEL), row),
        out_shape=jax.ShapeDtypeStruct((T, D_MODEL), F32),
        compiler_params=_cparams(("parallel",)),
        name="ffn",
    )(x2, wg, wu, wd, g, b)


def _od_in_kernel(x_ref, w_ref, wkt_ref, o_ref, kt_ref):
    xb = x_ref[...].astype(BF16)
    o_ref[...] = _dot(xb, w_ref[...])
    kt_ref[...] = _dot_nt(wkt_ref[...], xb) * (ML_HEAD_DIM ** -0.5)


def _od_in(x2, w, wkt, seq, tm=512):
    T = x2.shape[0]
    nps = seq // tm
    row = lambda i: (i, 0)
    return pl.pallas_call(
        _od_in_kernel,
        grid=(T // tm,),
        in_specs=[pl.BlockSpec((tm, D_MODEL), row), _const_spec(w.shape), _const_spec(wkt.shape)],
        out_specs=[pl.BlockSpec((tm, OD_COLS), row),
                   pl.BlockSpec((ML_INNER, tm), lambda i: (i // nps, i % nps))],
        out_shape=[jax.ShapeDtypeStruct((T, OD_COLS), F32),
                   jax.ShapeDtypeStruct((T // seq * ML_INNER, seq), F32)],
        compiler_params=_cparams(("parallel",)),
        name="od_in",
    )(x2, w, wkt)


def _tri(reverse):
    i = lax.broadcasted_iota(jnp.int32, (CHUNK, CHUNK), 0)
    j = lax.broadcasted_iota(jnp.int32, (CHUNK, CHUNK), 1)
    mask = (j >= i) if reverse else (j <= i)
    return mask, mask.astype(F32)


def _softplus(v):
    return jnp.maximum(v, 0.0) + jnp.log1p(jnp.exp(-jnp.abs(v)))


def _ssd_body(xbc_ref, sm_ref, bias_ref, a_ref, y_ref, st_ref, *, reverse, lane0):
    mask, tri = _tri(reverse)
    last = 0 if reverse else CHUNK - 1
    dt_all = _softplus(sm_ref[...] + bias_ref[...])
    da_all = dt_all * a_ref[...]
    cs_all = jnp.dot(tri, da_all, preferred_element_type=F32, precision=lax.Precision.HIGHEST)
    cs_t = cs_all.T
    tot_all = cs_all[last:last + 1, :]
    grow_all = jnp.exp(cs_all)
    rest_all = jnp.exp(tot_all - cs_all)
    etot_all = jnp.exp(tot_all)
    P = SSD_HEAD_DIM
    lo = lax.broadcasted_iota(jnp.int32, (CHUNK, 2 * P), 1) < P
    lo_row = lo[0:1, :]
    pairs_per_group = SSD_HEADS // SSD_GROUPS // 2
    prev = [st_ref[p] for p in range(SSD_HEADS // 2)]

    def pick(arr, l0):
        return jnp.where(lo if arr.shape[0] > 1 else lo_row, arr[:, l0:l0 + 1], arr[:, l0 + 1:l0 + 2])

    for g in range(SSD_GROUPS):
        b0 = SSD_INNER + g * SSD_STATE
        c0 = SSD_INNER + SSD_GROUPS * SSD_STATE + g * SSD_STATE
        cm = xbc_ref[:, c0:c0 + SSD_STATE].astype(BF16)
        bm_t = xbc_ref[:, b0:b0 + SSD_STATE].T.astype(BF16)
        cb = _dot(cm, bm_t)
        for pp in range(pairs_per_group):
            p = g * pairs_per_group + pp
            l0 = lane0 + 2 * p
            ps = slice(p * 2 * P, (p + 1) * 2 * P)
            xdt = xbc_ref[:, ps] * pick(dt_all, l0)
            x_lo = jnp.where(lo, xdt, 0.0).astype(BF16)
            x_hi = jnp.where(lo, 0.0, xdt).astype(BF16)
            dec0 = jnp.exp(jnp.where(mask, cs_all[:, l0:l0 + 1] - cs_t[l0:l0 + 1, :], -jnp.inf))
            dec1 = jnp.exp(jnp.where(mask, cs_all[:, l0 + 1:l0 + 2] - cs_t[l0 + 1:l0 + 2, :], -jnp.inf))
            y_diag = _dot((cb * dec0).astype(BF16), x_lo) + _dot((cb * dec1).astype(BF16), x_hi)
            y_off = _dot(cm, prev[p].astype(BF16)) * pick(grow_all, l0)
            y_ref[:, ps] = y_diag + y_off
            xw = (xdt * pick(rest_all, l0)).astype(BF16)
            st_ref[p] = pick(etot_all, l0) * prev[p] + _dot(bm_t, xw)


def _running_max(x, reverse):
    n = x.shape[0]
    row = lax.broadcasted_iota(jnp.int32, x.shape, 0)
    s = 1
    while s < n:
        if reverse:
            sh = jnp.where(row < n - s, pltpu.roll(x, n - s, 0), -jnp.inf)
        else:
            sh = jnp.where(row >= s, pltpu.roll(x, s, 0), -jnp.inf)
        x = jnp.maximum(x, sh)
        s *= 2
    return x


def _mlstm_body(q_ref, kt_ref, v_ref, ig_ref, fg_ref, ib_ref, fb_ref, h_ref, cn_ref, m_ref, *, reverse, direction):
    L = CHUNK
    mask, tri = _tri(reverse)
    last = 0 if reverse else L - 1
    li = ig_ref[...] + ib_ref[...]
    pre = fg_ref[...] + fb_ref[...]
    lf = jnp.minimum(pre, 0.0) - jnp.log1p(jnp.exp(-jnp.abs(pre)))
    bc = jnp.dot(tri, lf, preferred_element_type=F32, precision=lax.Precision.HIGHEST)
    u = li - bc
    cm = _running_max(u, reverse)
    m_prev = m_ref[...]
    big_m = jnp.maximum(m_prev, cm)
    g = bc[last:last + 1, :]
    m_loc = g + cm[last:last + 1, :]
    u_t = u.T
    e_end_t = jnp.exp(g + u - m_loc).T
    m_new = jnp.maximum(g + m_prev, m_loc)
    a_old = jnp.exp(g + m_prev - m_new)
    a_new = jnp.exp(m_loc - m_new)
    w_inter = jnp.exp(m_prev - big_m)
    emt = jnp.exp(-(bc + big_m))
    m_ref[...] = m_new

    lane = lax.broadcasted_iota(jnp.int32, (L, 128), 1)
    lo = lane < ML_HEAD_DIM
    row = lax.broadcasted_iota(jnp.int32, (128, 2 * 128), 0)
    col = lax.broadcasted_iota(jnp.int32, (128, 2 * 128), 1)
    block_diag = (row < ML_HEAD_DIM) == ((col % 128) < ML_HEAD_DIM)
    row_lo = lax.broadcasted_iota(jnp.int32, (128, L), 0) < ML_HEAD_DIM
    ones = jnp.ones((L, 128), F32)
    npairs = ML_HEADS // 2
    prev = [cn_ref[p] for p in range(npairs)]
    for p in range(npairs):
        l0 = direction * ML_HEADS + 2 * p
        l1 = l0 + 1
        ps = slice(p * 128, (p + 1) * 128)
        qp = q_ref[:, ps]
        vp = v_ref[:, ps]
        kt = kt_ref[ps, :]
        ktb = kt.astype(BF16)
        q_lo = jnp.where(lo, qp, 0.0).astype(BF16)
        q_hi = jnp.where(lo, 0.0, qp).astype(BF16)
        vo_lo = jnp.concatenate([jnp.where(lo, vp, 0.0), jnp.where(lo, ones, 0.0)], -1).astype(BF16)
        vo_hi = jnp.concatenate([jnp.where(lo, 0.0, vp), jnp.where(lo, 0.0, ones)], -1).astype(BF16)
        w0 = jnp.where(mask, jnp.exp(u_t[l0:l0 + 1, :] - big_m[:, l0:l0 + 1]), 0.0)
        w1 = jnp.where(mask, jnp.exp(u_t[l1:l1 + 1, :] - big_m[:, l1:l1 + 1]), 0.0)
        a0 = (_dot(q_lo, ktb) * w0).astype(BF16)
        a1 = (_dot(q_hi, ktb) * w1).astype(BF16)
        wi = jnp.where(lo, w_inter[:, l0:l0 + 1], w_inter[:, l1:l1 + 1])
        wi2 = jnp.concatenate([wi, wi], -1)
        nd = _dot(a0, vo_lo) + _dot(a1, vo_hi) + wi2 * _dot(qp.astype(BF16), prev[p].astype(BF16))
        floor = jnp.where(lo, emt[:, l0:l0 + 1], emt[:, l1:l1 + 1])
        h_ref[:, ps] = nd[:, :128] / jnp.maximum(jnp.abs(nd[:, 128:]), floor)
        e_t = jnp.where(row_lo, e_end_t[l0:l0 + 1, :], e_end_t[l1:l1 + 1, :])
        kte = (kt * e_t).astype(BF16)
        vo = jnp.concatenate([vp, ones], -1).astype(BF16)
        s_loc = jnp.where(block_diag, _dot(kte, vo), 0.0)
        row2 = lax.broadcasted_iota(jnp.int32, (128, 1), 0) < ML_HEAD_DIM
        ao = jnp.where(row2, a_old[:, l0:l0 + 1], a_old[:, l1:l1 + 1])
        an = jnp.where(row2, a_new[:, l0:l0 + 1], a_new[:, l1:l1 + 1])
        cn_ref[p] = ao * prev[p] + an * s_loc


def _odd_mix_kernel(xbc_f, dt_f, q_f, kt_f, v_f, ig_f, fg_f, xbc_r, dt_r, q_r, kt_r, v_r, ig_r, fg_r,
                    dtb_f, a_f, dtb_r, a_r, ib_ref, fb_ref, yf_ref, yb_ref, hf_ref, hb_ref,
                    st_f, st_r, cn_f, cn_r, m_f, m_r):
    @pl.when(pl.program_id(1) == 0)
    def _():
        for ref in (st_f, st_r, cn_f, cn_r, m_f, m_r):
            ref[...] = jnp.zeros(ref.shape, F32)

    _ssd_body(xbc_f, dt_f, dtb_f, a_f, yf_ref, st_f, reverse=False, lane0=0)
    _ssd_body(xbc_r, dt_r, dtb_r, a_r, yb_ref, st_r, reverse=True, lane0=SSD_HEADS)
    _mlstm_body(q_f, kt_f, v_f, ig_f, fg_f, ib_ref, fb_ref, hf_ref, cn_f, m_f, reverse=False, direction=0)
    _mlstm_body(q_r, kt_r, v_r, ig_r, fg_r, ib_ref, fb_ref, hb_ref, cn_r, m_r, reverse=True, direction=1)


def _odd_mix(xbc, proj, kt, S, w):
    T = proj.shape[0]
    B = T // S
    nc = S // CHUNK
    fwd = lambda c: c
    rev = lambda c: nc - 1 - c

    def specs(cidx):
        col = lambda cb: (lambda b, c: (b * nc + cidx(c), cb))
        return [
            pl.BlockSpec((CHUNK, SSD_XBC), col(0)),
            pl.BlockSpec((CHUNK, 128), col(OD_DT // 128)),
            pl.BlockSpec((CHUNK, ML_INNER), col(OD_Q // 512)),
            pl.BlockSpec((ML_INNER, CHUNK), lambda b, c: (b, cidx(c))),
            pl.BlockSpec((CHUNK, ML_INNER), col(OD_V // 512)),
            pl.BlockSpec((CHUNK, 128), col(OD_IG // 128)),
            pl.BlockSpec((CHUNK, 128), col(OD_FG // 128)),
        ]

    rows = [w["dt_bias"][0], w["a"][0], w["dt_bias"][1], w["a"][1], w["ig_b"], w["fg_b"]]
    out = lambda cidx: pl.BlockSpec((CHUNK, 512), lambda b, c: (b * nc + cidx(c), 0))
    seq = (xbc, proj, proj, kt, proj, proj, proj)
    return pl.pallas_call(
        _odd_mix_kernel,
        grid=(B, nc),
        in_specs=specs(fwd) + specs(rev) + [_const_spec(r.shape) for r in rows],
        out_specs=[out(fwd), out(rev), out(fwd), out(rev)],
        out_shape=[jax.ShapeDtypeStruct((T, 512), F32)] * 4,
        scratch_shapes=[
            pltpu.VMEM((SSD_HEADS // 2, SSD_STATE, 2 * SSD_HEAD_DIM), F32),
            pltpu.VMEM((SSD_HEADS // 2, SSD_STATE, 2 * SSD_HEAD_DIM), F32),
            pltpu.VMEM((ML_HEADS // 2, 128, 256), F32),
            pltpu.VMEM((ML_HEADS // 2, 128, 256), F32),
            pltpu.VMEM((1, 128), F32),
            pltpu.VMEM((1, 128), F32),
        ],
        compiler_params=_cparams(("parallel", "arbitrary")),
        name="odd_mix",
    )(*seq, *seq, *rows)


def _od_out_kernel(x_ref, z_ref, xs_ref, o_ref, yf_ref, yb_ref, hf_ref, hb_ref, dsk_ref, sg_ref, mg_ref, avg_ref,
                   w1_ref, w2_ref, g_ref, b_ref, rw_ref, rb_ref, x1_ref, route_ref, route_t_ref):
    y = (yf_ref[...] + yb_ref[...] + xs_ref[...] * dsk_ref[...]) * _silu(z_ref[...])
    gw = SSD_INNER // SSD_GROUPS
    m = None
    for g in range(SSD_GROUPS):
        sl = slice(g * gw, (g + 1) * gw)
        yn = _rmsnorm(y[:, sl], sg_ref[:, sl]).astype(BF16)
        t = _dot(yn, w1_ref[sl, :])
        m = t if m is None else m + t
    hs = hf_ref[...] + hb_ref[...]
    avg = avg_ref[...]

    def head_mean(v):
        hi = v.astype(BF16)
        lo = (v - hi.astype(F32)).astype(BF16)
        return _dot(hi, avg) + _dot(lo, avg)

    dv = hs - head_mean(hs)
    var = head_mean(dv * dv)
    hn = jax.nn.sigmoid(o_ref[...]) * (dv * lax.rsqrt(var + LN_EPS) * mg_ref[...])
    m = m + _dot(hn.astype(BF16), w2_ref[...])
    x1 = _layernorm(ALPHA * x_ref[...] + m, g_ref[...], b_ref[...])
    x1_ref[...] = x1
    logits = _dot(x1.astype(BF16), rw_ref[...]) + rb_ref[...]
    lane = lax.broadcasted_iota(jnp.int32, logits.shape, 1).astype(F32)
    m1 = jnp.max(logits, -1, keepdims=True)
    i1 = jnp.min(jnp.where(logits == m1, lane, 128.0), -1, keepdims=True)
    rest = jnp.where(lane == i1, -jnp.inf, logits)
    m2 = jnp.max(rest, -1, keepdims=True)
    i2 = jnp.min(jnp.where(rest == m2, lane, 128.0), -1, keepdims=True)
    e = jnp.exp(m2 - m1)
    g1 = 1.0 / (1.0 + e)
    g2 = e / (1.0 + e)
    route = jnp.where(lane == 0.0, i1,
                      jnp.where(lane == 1.0, i2, jnp.where(lane == 2.0, g1, jnp.where(lane == 3.0, g2, 0.0))))
    route_ref[...] = route
    route_t_ref[...] = route.T[0:8, :]


def _od_out(x2, proj, xbc, yf, yb, hf, hb, dsk, sg, mg, w1, w2, g, b, rw, rb, tm=256):
    T = x2.shape[0]
    row = lambda i: (i, 0)
    colb = lambda cb: (lambda i: (i, cb))
    head = jnp.arange(ML_INNER) // ML_HEAD_DIM
    avg = jnp.where(head[:, None] == head[None, :], 1.0 / ML_HEAD_DIM, 0.0).astype(BF16)
    return pl.pallas_call(
        _od_out_kernel,
        grid=(T // tm,),
        in_specs=[
            pl.BlockSpec((tm, D_MODEL), row),
            pl.BlockSpec((tm, 512), colb(OD_Z // 512)),
            pl.BlockSpec((tm, 512), colb(0)),
            pl.BlockSpec((tm, 512), colb(OD_O // 512)),
            pl.BlockSpec((tm, 512), row), pl.BlockSpec((tm, 512), row),
            pl.BlockSpec((tm, 512), row), pl.BlockSpec((tm, 512), row),
            _const_spec(dsk.shape), _const_spec(sg.shape), _const_spec(mg.shape), _const_spec(avg.shape),
            _const_spec(w1.shape), _const_spec(w2.shape), _const_spec(g.shape), _const_spec(b.shape),
            _const_spec(rw.shape), _const_spec(rb.shape),
        ],
        out_specs=[pl.BlockSpec((tm, D_MODEL), row), pl.BlockSpec((tm, 128), row),
                   pl.BlockSpec((8, tm), lambda i: (0, i))],
        out_shape=[jax.ShapeDtypeStruct((T, D_MODEL), F32), jax.ShapeDtypeStruct((T, 128), F32),
                   jax.ShapeDtypeStruct((8, T), F32)],
        compiler_params=_cparams(("parallel",)),
        name="od_out",
    )(x2, proj, xbc, proj, yf, yb, hf, hb, dsk, sg, mg, avg, w1, w2, g, b, rw, rb)


def _row_copy(src_hbm, idx, dst, r, sem):
    return pltpu.make_async_copy(src_hbm.at[pl.ds(idx, 1)], dst.at[pl.ds(r, 1)], sem)


def _gather_rows(src_hbm, idx_smem, dst, sem, n):
    for r in range(n):
        _row_copy(src_hbm, idx_smem[0, r], dst, r, sem).start(priority=r % 2)


def _gather_wait(src_hbm, dst, sem):
    pltpu.make_async_copy(src_hbm.at[pl.ds(0, dst.shape[0])], dst, sem).wait()


def _gather_loop(src_hbm, idx_smem, idx0, dst, sem):
    def start(r, carry):
        _row_copy(src_hbm, idx_smem[0, idx0 + r], dst, r, sem).start()
        return carry

    lax.fori_loop(0, dst.shape[0], start, 0, unroll=8)


def _moe_kernel(te_ref, tok0_ref, tokn_ref, x_hbm, wg_ref, wu_ref, wd_ref, o_ref,
                xbuf, xb_ref, acc_ref, sem, *, nfc, sub):
    i = pl.program_id(0)
    j = pl.program_id(1)
    n = pl.num_programs(0)
    per_step = xbuf.shape[1]

    @pl.when(jnp.logical_and(i == 0, j == 0))
    def _():
        for jj in range(nfc):
            _gather_loop(x_hbm, tok0_ref, jj * per_step, xbuf.at[jj], sem)

    @pl.when(j == 0)
    def _():
        for jj in range(nfc):
            _gather_wait(x_hbm, xbuf.at[jj], sem)
        for jj in range(nfc):
            xb_ref[jj * per_step:(jj + 1) * per_step, :] = xbuf[jj].astype(BF16)
        acc_ref[...] = jnp.zeros(acc_ref.shape, F32)

    xb = xb_ref[...]
    acc = acc_ref[...]
    for c in range(wg_ref.shape[1] // sub):
        sl = slice(c * sub, (c + 1) * sub)
        hh = _silu(_dot(xb, wg_ref[:, sl])) * _dot(xb, wu_ref[:, sl])
        acc = acc + _dot(hh.astype(BF16), wd_ref[sl, :])
    acc_ref[...] = acc
    _gather_rows(x_hbm, tokn_ref, xbuf.at[j], sem, per_step)

    @pl.when(j == nfc - 1)
    def _():
        o_ref[...] = acc_ref[...]

    @pl.when(jnp.logical_and(i == n - 1, j == nfc - 1))
    def _():
        for jj in range(nfc):
            _gather_wait(x_hbm, xbuf.at[jj], sem)


def _moe(x1, tile_expert, row_token, wg, wu, wd, tm, fc=1792, sub=256):
    n_tiles = tile_expert.shape[0]
    nfc = D_FF_EXPERT // fc
    per_step = tm // nfc
    grid_spec = pltpu.PrefetchScalarGridSpec(
        num_scalar_prefetch=1,
        grid=(n_tiles, nfc),
        in_specs=[
            pl.BlockSpec((None, 1, tm), lambda i, j, te: (0, 0, 0), memory_space=pltpu.SMEM),
            pl.BlockSpec((None, 1, per_step), lambda i, j, te: ((i + 1) * nfc + j, 0, 0), memory_space=pltpu.SMEM),
            pl.BlockSpec(memory_space=pl.ANY),
            pl.BlockSpec((None, D_MODEL, fc), lambda i, j, te: (te[i], 0, j)),
            pl.BlockSpec((None, D_MODEL, fc), lambda i, j, te: (te[i], 0, j)),
            pl.BlockSpec((None, fc, D_MODEL), lambda i, j, te: (te[i], j, 0)),
        ],
        out_specs=pl.BlockSpec((tm, D_MODEL), lambda i, j, te: (i, 0)),
        scratch_shapes=[
            pltpu.VMEM((nfc, per_step, D_MODEL), F32),
            pltpu.VMEM((tm, D_MODEL), BF16),
            pltpu.VMEM((tm, D_MODEL), F32),
            pltpu.SemaphoreType.DMA,
        ],
    )
    return pl.pallas_call(
        functools.partial(_moe_kernel, nfc=nfc, sub=sub),
        grid_spec=grid_spec,
        out_shape=jax.ShapeDtypeStruct((n_tiles * tm, D_MODEL), F32),
        compiler_params=_cparams(("arbitrary", "arbitrary")),
        name="moe_experts",
    )(tile_expert, row_token.reshape(n_tiles + 1, 1, tm), row_token.reshape((n_tiles + 1) * nfc, 1, per_step),
      x1, wg, wu, wd)


def _combine_kernel(pos0_ref, posa_ref, posb_ref, x_ref, route_ref, y_hbm, g_ref, b_ref, o_ref, ybuf, sem, *, tm):
    s = pl.program_id(0)
    n = pl.num_programs(0)

    @pl.when(s == 0)
    def _():
        for half in range(2):
            _gather_loop(y_hbm, pos0_ref, half * 2 * tm, ybuf.at[half], sem.at[half])

    for half, pos_ref in ((0, posa_ref), (1, posb_ref)):
        _gather_wait(y_hbm, ybuf.at[half], sem.at[half])
        rows = slice(half * tm, (half + 1) * tm)
        g0 = route_ref[rows, 2:3]
        g1 = route_ref[rows, 3:4]
        f = g0 * ybuf[half, 0:tm, :] + g1 * ybuf[half, tm:2 * tm, :]
        _gather_rows(y_hbm, pos_ref, ybuf.at[half], sem.at[half], 2 * tm)
        o_ref[rows, :] = _layernorm(ALPHA * x_ref[rows, :] + f, g_ref[...], b_ref[...])

    @pl.when(s == n - 1)
    def _():
        _gather_wait(y_hbm, ybuf.at[0], sem.at[0])
        _gather_wait(y_hbm, ybuf.at[1], sem.at[1])


def _combine(x1, route, y_sorted, pos, g, b, tm=256):
    T = route.shape[0]
    nt = T // tm
    assert nt % 2 == 0
    return pl.pallas_call(
        functools.partial(_combine_kernel, tm=tm),
        grid=(nt // 2,),
        in_specs=[
            pl.BlockSpec((None, 1, 4 * tm), lambda s: (0, 0, 0), memory_space=pltpu.SMEM),
            pl.BlockSpec((None, 1, 2 * tm), lambda s: (2 * s + 2, 0, 0), memory_space=pltpu.SMEM),
            pl.BlockSpec((None, 1, 2 * tm), lambda s: (2 * s + 3, 0, 0), memory_space=pltpu.SMEM),
            pl.BlockSpec((2 * tm, D_MODEL), lambda s: (s, 0)),
            pl.BlockSpec((2 * tm, 128), lambda s: (s, 0)),
            pl.BlockSpec(memory_space=pl.ANY),
            _const_spec(g.shape), _const_spec(b.shape),
        ],
        out_specs=pl.BlockSpec((2 * tm, D_MODEL), lambda s: (s, 0)),
        out_shape=jax.ShapeDtypeStruct((T, D_MODEL), F32),
        scratch_shapes=[pltpu.VMEM((2, 2 * tm, D_MODEL), F32), pltpu.SemaphoreType.DMA((2,))],
        compiler_params=_cparams(("arbitrary",)),
        name="moe_combine",
    )(pos.reshape((nt + 2) // 2, 1, 4 * tm), pos, pos, x1, route, y_sorted, g, b)


def _route_tables(route_t, tm_e, tm_c):
    T = route_t.shape[1]
    A = 2 * T
    e0 = route_t[0].astype(jnp.int32)
    e1 = route_t[1].astype(jnp.int32)
    ids = jnp.arange(N_EXPERTS, dtype=jnp.int32)[:, None]
    oh0 = e0[None, :] == ids
    oh1 = e1[None, :] == ids
    c0 = jnp.cumsum(oh0.astype(jnp.int32), axis=1)
    c1 = jnp.cumsum(oh1.astype(jnp.int32), axis=1)
    n0 = c0[:, -1:]
    counts = (n0 + c1[:, -1:])[:, 0]
    start = jnp.cumsum(counts) - counts
    padded = ((counts + tm_e - 1) // tm_e) * tm_e
    pend = jnp.cumsum(padded)
    pstart = pend - padded
    dest0 = jnp.sum(jnp.where(oh0, pstart[:, None] + c0 - 1, 0), axis=0)
    dest1 = jnp.sum(jnp.where(oh1, pstart[:, None] + n0 + c1 - 1, 0), axis=0)
    order = jnp.argsort(jnp.concatenate([e0, e1]), stable=True).astype(jnp.int32)
    n_tiles = A // tm_e + N_EXPERTS
    rows = jnp.arange(n_tiles * tm_e, dtype=jnp.int32)
    past = rows[None, :] >= pend[:, None]
    e_row = jnp.minimum(jnp.sum(past.astype(jnp.int32), axis=0), N_EXPERTS - 1)
    local = rows - jnp.sum(jnp.where(past, padded[:, None], 0), axis=0)
    count_row = jnp.sum(jnp.where(e_row[None, :] == ids, counts[:, None], 0), axis=0)
    ok = jnp.logical_and(local < count_row, rows < pend[-1])
    src = jnp.sum(jnp.where(past, counts[:, None], 0), axis=0) + local
    a_row = order[jnp.clip(src, 0, A - 1)]
    row_token = jnp.where(ok, jnp.where(a_row >= T, a_row - T, a_row), 0).astype(jnp.int32)
    row_token = jnp.concatenate([row_token, jnp.zeros((tm_e,), jnp.int32)])
    tile_expert = e_row[::tm_e]
    nt = T // tm_c
    pos = jnp.concatenate([dest0.reshape(nt, tm_c), dest1.reshape(nt, tm_c)], axis=1)
    pos = jnp.concatenate([pos, jnp.zeros((2, 2 * tm_c), jnp.int32)], 0).astype(jnp.int32)
    return tile_expert, row_token, pos.reshape(nt + 2, 1, 2 * tm_c)


def _rot_cols(w):
    half = MLA_ROPE // 2
    return jnp.concatenate([-w[..., half:], w[..., :half]], -1)


def _prep_even(p, j):
    w_in = p["ev_w_in"][j]
    c_rot = 2 * CONV_CH + MLA_Q_LORA + MLA_KV_LORA
    k_rot = w_in[:, c_rot:c_rot + MLA_ROPE]
    z64 = jnp.zeros((D_MODEL, MLA_NOPE), F32)
    z32 = jnp.zeros((D_MODEL, HEAD_PAD - MLA_NOPE - MLA_ROPE), F32)
    w_in2 = jnp.concatenate([w_in[:, :c_rot], z64, k_rot, z32, z64, _rot_cols(k_rot), z32], -1).astype(BF16)
    wq = p["mla_w_uq"][j].reshape(MLA_Q_LORA, MLA_HEADS, MLA_NOPE + MLA_ROPE)
    zq = jnp.zeros((MLA_Q_LORA, MLA_HEADS, HEAD_PAD - MLA_NOPE - MLA_ROPE), F32)
    zq64 = jnp.zeros((MLA_Q_LORA, MLA_HEADS, MLA_NOPE), F32)
    wq_plain = jnp.concatenate([wq, zq], -1).reshape(MLA_Q_LORA, -1)
    wq_rot = jnp.concatenate([zq64, _rot_cols(wq[..., MLA_NOPE:]), zq], -1).reshape(MLA_Q_LORA, -1)
    wq2 = jnp.concatenate([wq_plain, wq_rot], -1).astype(BF16)
    wkv = p["mla_w_ukv"][j].reshape(MLA_KV_LORA, MLA_HEADS, MLA_NOPE + MLA_V)
    zk = jnp.zeros((MLA_KV_LORA, MLA_HEADS, MLA_V), F32)
    wk = jnp.concatenate([wkv[..., :MLA_NOPE], zk], -1).reshape(MLA_KV_LORA, -1)
    wv = wkv[..., MLA_NOPE:]
    even = (jnp.arange(MLA_HEADS) % 2 == 0)[None, :, None]
    wv2 = jnp.concatenate([jnp.where(even, wv, 0.0), jnp.where(even, 0.0, wv)], -1).reshape(MLA_KV_LORA, -1)
    wkv2 = jnp.concatenate([wk, wv2], -1).astype(BF16)
    w_out = p["ev_w_out"][j].astype(BF16)
    return dict(
        w_in=w_in2, qg=p["mla_q_norm_g"][j][None], wq=wq2, kvg=p["mla_kv_norm_g"][j][None], wkv=wkv2,
        dw_w=p["conv_dw_w"][j], dw_b=p["conv_dw_b"][j][None], cln_g=p["conv_ln_g"][j][None],
        cln_b=p["conv_ln_b"][j][None], w1=w_out[:CONV_CH], w2=w_out[CONV_CH:],
        wg=p["ffn_w_gate"][j].astype(BF16), wu=p["ffn_w_up"][j].astype(BF16), wd=p["ffn_w_down"][j].astype(BF16),
    )


def _lane_row(vals, lane0):
    return jnp.zeros((1, 128), F32).at[0, lane0:lane0 + vals.shape[0]].set(vals)


def _prep_odd(p, j):
    w = p["od_w_in"][j]
    z112 = jnp.zeros((D_MODEL, 128 - 16), F32)
    w_in2 = jnp.concatenate([w[:, 0:1536], w[:, 1552:2064], w[:, 2576:3600],
                             w[:, 1536:1552], z112, w[:, 3600:3616], z112, w[:, 3616:3632], z112], -1).astype(BF16)
    wkt = w[:, 2064:2576].T.astype(BF16)
    w_out = p["od_w_out"][j].astype(BF16)
    rw = jnp.concatenate([p["moe_router_w"][j], jnp.zeros((D_MODEL, 128 - N_EXPERTS), F32)], -1).astype(BF16)
    rb = jnp.full((1, 128), -jnp.inf, F32).at[0, :N_EXPERTS].set(p["moe_router_b"][j])
    a = -jnp.exp(p["ssd_a_log"][j])
    return dict(
        w_in=w_in2, wkt=wkt, cw=p["ssd_conv_w"][j], cb=p["ssd_conv_b"][j][None],
        dt_bias=[_lane_row(p["ssd_dt_bias"][j][d], d * SSD_HEADS) for d in range(2)],
        a=[_lane_row(a[d], d * SSD_HEADS) for d in range(2)],
        ig_b=_lane_row(p["ml_igate_b"][j].reshape(-1), 0), fg_b=_lane_row(p["ml_fgate_b"][j].reshape(-1), 0),
        dsk=jnp.repeat(p["ssd_d"][j], SSD_HEAD_DIM)[None], sg=p["ssd_norm_g"][j][None], mg=p["ml_norm_g"][j][None],
        w1=w_out[:SSD_INNER], w2=w_out[SSD_INNER:], rw=rw, rb=rb,
        wg=p["moe_w_gate"][j].astype(BF16), wu=p["moe_w_up"][j].astype(BF16), wd=p["moe_w_down"][j].astype(BF16),
    )


def _rope_tables(seq):
    half = MLA_ROPE // 2
    inv_freq = ROPE_THETA ** (-jnp.arange(half, dtype=F32) / half)
    ang = jnp.arange(seq, dtype=F32)[:, None] * inv_freq
    cos2 = jnp.concatenate([jnp.cos(ang), jnp.cos(ang)], -1)
    sin2 = jnp.concatenate([jnp.sin(ang), jnp.sin(ang)], -1)
    pad = jnp.zeros((seq, HEAD_PAD - MLA_NOPE - MLA_ROPE), F32)
    cos_t = jnp.concatenate([jnp.ones((seq, MLA_NOPE), F32), cos2, pad], -1)
    sin_t = jnp.concatenate([jnp.zeros((seq, MLA_NOPE), F32), sin2, pad], -1)
    return cos_t, sin_t


def _even_layer(x2, B, S, w, ln, cos_t, sin_t):
    tm = min(512, S)
    u, q, k, v = _ev_in(x2, cos_t, sin_t, w["w_in"], w["qg"], w["wq"], w["kvg"], w["wkv"], S, tm=tm)
    uc = _dwconv(u, S, 0, w["dw_w"], w["dw_b"], w["cln_g"], w["cln_b"], with_ln=True, out_dtype=BF16)
    att = _attention(q, k, v, B, S, tq=min(256, S))
    x1 = _ev_out(x2, uc, att, w["w1"], w["w2"], ln[0], ln[1], tm=tm)
    return _ffn(x1, w["wg"], w["wu"], w["wd"], ln[2], ln[3], tm=tm)


def _odd_layer(x2, B, S, w, ln, tm_e=512, tm_c=256):
    proj, kt = _od_in(x2, w["w_in"], w["wkt"], S, tm=min(512, S))
    zeros = jnp.zeros((1, SSD_XBC), F32)
    xbc = _dwconv(proj, S, OD_XBC // 512, w["cw"], w["cb"], zeros, zeros, with_ln=False, out_dtype=F32, ncb=2)
    yf, yb, hf, hb = _odd_mix(xbc, proj, kt, S, w)
    x1, route, route_t = _od_out(x2, proj, xbc, yf, yb, hf, hb, w["dsk"], w["sg"], w["mg"],
                                 w["w1"], w["w2"], ln[0], ln[1], w["rw"], w["rb"])
    te, row_token, pos = _route_tables(route_t, tm_e, tm_c)
    y_sorted = _moe(x1, te, row_token, w["wg"], w["wu"], w["wd"], tm_e)
    return _combine(x1, route, y_sorted, pos, ln[2], ln[3], tm_c)


def _trunk(x, p):
    B, S, _ = x.shape
    x2 = x.reshape(B * S, D_MODEL)
    cos_t, sin_t = _rope_tables(S)
    for l in range(DEPTH):
        j = l // 2
        ln = (p["ln1_g"][l][None], p["ln1_b"][l][None], p["ln2_g"][l][None], p["ln2_b"][l][None])
        if l % 2 == 0:
            x2 = _even_layer(x2, B, S, _prep_even(p, j), ln, cos_t, sin_t)
        else:
            x2 = _odd_layer(x2, B, S, _prep_odd(p, j), ln)
    return x2.reshape(B, S, D_MODEL)


def kernel(x_prompt, x_sample, ev_w_in, conv_dw_w, conv_dw_b, conv_ln_g, conv_ln_b, mla_q_norm_g, mla_w_uq, mla_kv_norm_g, mla_w_ukv, ev_w_out, od_w_in, ssd_conv_w, ssd_conv_b, ssd_dt_bias, ssd_a_log, ssd_d, ssd_norm_g, ml_igate_b, ml_fgate_b, ml_norm_g, od_w_out, ffn_w_gate, ffn_w_up, ffn_w_down, moe_router_w, moe_router_b, moe_w_gate, moe_w_up, moe_w_down, ln1_g, ln1_b, ln2_g, ln2_b):
    p = dict(ev_w_in=ev_w_in, conv_dw_w=conv_dw_w, conv_dw_b=conv_dw_b, conv_ln_g=conv_ln_g, conv_ln_b=conv_ln_b,
             mla_q_norm_g=mla_q_norm_g, mla_w_uq=mla_w_uq, mla_kv_norm_g=mla_kv_norm_g, mla_w_ukv=mla_w_ukv,
             ev_w_out=ev_w_out, od_w_in=od_w_in, ssd_conv_w=ssd_conv_w, ssd_conv_b=ssd_conv_b,
             ssd_dt_bias=ssd_dt_bias, ssd_a_log=ssd_a_log, ssd_d=ssd_d, ssd_norm_g=ssd_norm_g,
             ml_igate_b=ml_igate_b, ml_fgate_b=ml_fgate_b, ml_norm_g=ml_norm_g, od_w_out=od_w_out,
             ffn_w_gate=ffn_w_gate, ffn_w_up=ffn_w_up, ffn_w_down=ffn_w_down, moe_router_w=moe_router_w,
             moe_router_b=moe_router_b, moe_w_gate=moe_w_gate, moe_w_up=moe_w_up, moe_w_down=moe_w_down,
             ln1_g=ln1_g, ln1_b=ln1_b, ln2_g=ln2_g, ln2_b=ln2_b)
    assert x_prompt.shape[1] == x_sample.shape[1]
    nb = x_prompt.shape[0]
    y = _trunk(jnp.concatenate([x_prompt, x_sample], 0), p)
    return (y[:nb], y[nb:])
```

```python
import functools
import math

import jax
import jax.numpy as jnp
import numpy as np
from jax import lax
from jax.experimental import pallas as pl
from jax.experimental.pallas import tpu as pltpu

F32 = jnp.float32
BF16 = jnp.bfloat16

D_MODEL = 1024
DEPTH = 4
ALPHA = (2.0 * DEPTH) ** 0.25
LN_EPS = 1e-5
RMS_EPS = 1e-6

CONV_CH = 512
CONV_W = 31
MLA_HEADS = 8
MLA_NOPE = 64
MLA_ROPE = 32
MLA_V = 64
MLA_Q_LORA = 256
MLA_KV_LORA = 128
ROPE_THETA = 10000.0
HEAD_PAD = 128
EV_COLS = 2 * CONV_CH + MLA_Q_LORA + MLA_KV_LORA + 2 * HEAD_PAD
Q_SCALE = (MLA_NOPE + MLA_ROPE) ** -0.5 * math.log2(math.e)

SSD_HEADS = 8
SSD_HEAD_DIM = 64
SSD_INNER = 512
SSD_GROUPS = 2
SSD_STATE = 128
SSD_CONV_W = 5
SSD_XBC = 1024
CHUNK = 128
ML_HEADS = 8
ML_HEAD_DIM = 64
ML_INNER = 512
OD_Z, OD_XBC, OD_Q, OD_V, OD_O, OD_DT, OD_IG, OD_FG = 0, 512, 1536, 2048, 2560, 3072, 3200, 3328
OD_COLS = 3456

D_FF = 2816
N_EXPERTS = 8
D_FF_EXPERT = 3584

VMEM_LIMIT = 56 * 1024 * 1024


def _cparams(sem):
    return pltpu.CompilerParams(dimension_semantics=sem, vmem_limit_bytes=VMEM_LIMIT)


def _const_spec(shape):
    nd = len(shape)
    return pl.BlockSpec(shape, lambda *_: (0,) * nd, pipeline_mode=pl.Buffered(1))


def _layernorm(v, g, b):
    mu = jnp.mean(v, -1, keepdims=True)
    d = v - mu
    var = jnp.mean(d * d, -1, keepdims=True)
    return d * lax.rsqrt(var + LN_EPS) * g + b


def _rmsnorm(v, g):
    return v * lax.rsqrt(jnp.mean(v * v, -1, keepdims=True) + RMS_EPS) * g


def _silu(v):
    return v * jax.nn.sigmoid(v)


def _dot(a, b):
    return jnp.dot(a, b, preferred_element_type=F32)


def _dot_nt(a, b):
    return lax.dot_general(a, b, (((1,), (1,)), ((), ())), preferred_element_type=F32)


def _ev_in_kernel(x_ref, cos_ref, sin_ref, w_in_ref, qg_ref, wq_ref, kvg_ref, wkv_ref,
                  u_ref, q_ref, k_ref, v_ref):
    xb = x_ref[...].astype(BF16)
    h = _dot(xb, w_in_ref[...])
    u_ref[...] = h[:, :CONV_CH] * jax.nn.sigmoid(h[:, CONV_CH:2 * CONV_CH])
    c0 = 2 * CONV_CH
    cos = cos_ref[...]
    sin = sin_ref[...]
    ql = _rmsnorm(h[:, c0:c0 + MLA_Q_LORA], qg_ref[...]).astype(BF16)
    qq = _dot(ql, wq_ref[...])
    c1 = c0 + MLA_Q_LORA
    kvl = _rmsnorm(h[:, c1:c1 + MLA_KV_LORA], kvg_ref[...]).astype(BF16)
    kk = _dot(kvl, wkv_ref[...])
    c2 = c1 + MLA_KV_LORA
    kpe = h[:, c2:c2 + HEAD_PAD] * cos + h[:, c2 + HEAD_PAD:c2 + 2 * HEAD_PAD] * sin
    nh = MLA_HEADS * HEAD_PAD
    for hd in range(MLA_HEADS):
        sl = slice(hd * HEAD_PAD, (hd + 1) * HEAD_PAD)
        sl2 = slice(nh + hd * HEAD_PAD, nh + (hd + 1) * HEAD_PAD)
        q_ref[:, sl] = ((qq[:, sl] * cos + qq[:, sl2] * sin) * Q_SCALE).astype(BF16)
        k_ref[:, sl] = (kk[:, sl] + kpe).astype(BF16)
    v_ref[...] = kk[:, nh:].astype(BF16)


def _ev_in(x2, cos_t, sin_t, w_in, qg, wq, kvg, wkv, seq, tm=512):
    T = x2.shape[0]
    nps = seq // tm
    row = lambda i: (i, 0)
    pos = lambda i: (i % nps, 0)
    nh = MLA_HEADS * HEAD_PAD
    return pl.pallas_call(
        _ev_in_kernel,
        grid=(T // tm,),
        in_specs=[
            pl.BlockSpec((tm, D_MODEL), row),
            pl.BlockSpec((tm, HEAD_PAD), pos),
            pl.BlockSpec((tm, HEAD_PAD), pos),
            _const_spec(w_in.shape), _const_spec(qg.shape), _const_spec(wq.shape),
            _const_spec(kvg.shape), _const_spec(wkv.shape),
        ],
        out_specs=[
            pl.BlockSpec((tm, CONV_CH), row),
            pl.BlockSpec((tm, nh), row),
            pl.BlockSpec((tm, nh), row),
            pl.BlockSpec((tm, nh), row),
        ],
        out_shape=[
            jax.ShapeDtypeStruct((T, CONV_CH), F32),
            jax.ShapeDtypeStruct((T, nh), BF16),
            jax.ShapeDtypeStruct((T, nh), BF16),
            jax.ShapeDtypeStruct((T, nh), BF16),
        ],
        compiler_params=_cparams(("parallel",)),
        name="ev_in",
    )(x2, cos_t, sin_t, w_in, qg, wq, kvg, wkv)


def _dwconv_kernel(x_ref, w_ref, b_ref, g_ref, beta_ref, o_ref, pad_ref, tmp_ref, *, width, halo, rows, with_ln):
    S, C = x_ref.shape
    half = width // 2
    win = rows + 2 * halo
    pad_ref[0:halo, :] = jnp.zeros((halo, C), F32)
    pad_ref[halo + S:halo + S + halo, :] = jnp.zeros((halo, C), F32)
    pad_ref[halo:halo + S, :] = x_ref[...]

    def tile(t, carry):
        r0 = pl.multiple_of(t * rows, rows)
        for cb in range(C // 128):
            cs = slice(cb * 128, (cb + 1) * 128)
            window = pad_ref[pl.ds(r0, win), cs]
            acc = jnp.zeros((rows, 128), F32) + b_ref[:, cs]
            for r in range(8):
                taps = [w for w in range(width) if (halo + w - half) % 8 == r]
                if taps:
                    rolled = window if r == 0 else pltpu.roll(window, win - r, 0)
                    for w in taps:
                        a0 = halo + w - half - r
                        acc = acc + rolled[a0:a0 + rows] * w_ref[w:w + 1, cs]
            tmp_ref[:, cs] = acc
        acc = tmp_ref[...]
        if with_ln:
            acc = _layernorm(acc, g_ref[...], beta_ref[...])
        o_ref[pl.ds(r0, rows), :] = _silu(acc).astype(o_ref.dtype)
        return carry

    lax.fori_loop(0, S // rows, tile, 0)


def _dwconv(x2, S, col_block, w, b, g, beta, *, with_ln, out_dtype, ncb=1, rows=128):
    B = x2.shape[0] // S
    C = 512
    width = w.shape[0]
    halo = 16
    assert width // 2 <= halo
    kern = functools.partial(_dwconv_kernel, width=width, halo=halo, rows=rows, with_ln=with_ln)
    return pl.pallas_call(
        kern,
        grid=(B, ncb),
        in_specs=[
            pl.BlockSpec((S, C), lambda b, c: (b, col_block + c)),
            pl.BlockSpec((width, C), lambda b, c: (0, c)),
            pl.BlockSpec((1, C), lambda b, c: (0, c)),
            pl.BlockSpec((1, C), lambda b, c: (0, c)),
            pl.BlockSpec((1, C), lambda b, c: (0, c)),
        ],
        out_specs=pl.BlockSpec((S, C), lambda b, c: (b, c)),
        out_shape=jax.ShapeDtypeStruct((B * S, C * ncb), out_dtype),
        scratch_shapes=[pltpu.VMEM((S + 2 * halo, C), F32), pltpu.VMEM((rows, C), F32)],
        compiler_params=_cparams(("parallel", "parallel")),
        name="dwconv_ln" if with_ln else "dwconv",
    )(x2, w, b, g, beta)


ATT_HEADS_PER_STEP = 8


def _attn_kernel(q_ref, k_ref, v_ref, o_ref):
    for pair in range(ATT_HEADS_PER_STEP // 2):
        acc = None
        for j in range(2):
            hd = 2 * pair + j
            sl = slice(hd * HEAD_PAD, (hd + 1) * HEAD_PAD)
            s = _dot_nt(q_ref[:, sl], k_ref[:, sl])
            m = jnp.max(s, -1, keepdims=True)
            p = jnp.exp2(s - m)
            l = jnp.sum(p, -1, keepdims=True)
            o = _dot(p.astype(BF16), v_ref[:, sl]) / l
            acc = o if acc is None else acc + o
        o_ref[:, pair * 2 * MLA_V:(pair + 1) * 2 * MLA_V] = acc.astype(o_ref.dtype)


def _attention(q, k, v, B, S, tq=256):
    nq = S // tq
    T = B * S
    hps = ATT_HEADS_PER_STEP
    return pl.pallas_call(
        _attn_kernel,
        grid=(B, MLA_HEADS // hps, nq),
        in_specs=[
            pl.BlockSpec((tq, hps * HEAD_PAD), lambda b, hp, i: (b * nq + i, hp)),
            pl.BlockSpec((S, hps * HEAD_PAD), lambda b, hp, i: (b, hp), pipeline_mode=pl.Buffered(1)),
            pl.BlockSpec((S, hps * HEAD_PAD), lambda b, hp, i: (b, hp), pipeline_mode=pl.Buffered(1)),
        ],
        out_specs=pl.BlockSpec((tq, hps * MLA_V), lambda b, hp, i: (b * nq + i, hp)),
        out_shape=jax.ShapeDtypeStruct((T, MLA_HEADS * MLA_V), BF16),
        compiler_params=_cparams(("parallel", "parallel", "parallel")),
        name="attention",
    )(q, k, v)


def _ev_out_ffn_kernel(x_ref, u_ref, a_ref, w1_ref, w2_ref, g1_ref, b1_ref, wg_ref, wu_ref, wd_ref, g2_ref, b2_ref,
                       o_ref, *, fc):
    m = _dot(u_ref[...], w1_ref[...]) + _dot(a_ref[...], w2_ref[...])
    x = _layernorm(ALPHA * x_ref[...] + m, g1_ref[...], b1_ref[...])
    xb = x.astype(BF16)
    acc = jnp.zeros(x.shape, F32)
    for c in range(wg_ref.shape[1] // fc):
        sl = slice(c * fc, (c + 1) * fc)
        hh = _silu(_dot(xb, wg_ref[:, sl])) * _dot(xb, wu_ref[:, sl])
        acc = acc + _dot(hh.astype(BF16), wd_ref[sl, :])
    o_ref[...] = _layernorm(ALPHA * x + acc, g2_ref[...], b2_ref[...])


def _ev_out_ffn(x2, u, att, w1, w2, g1, b1, wg, wu, wd, g2, b2, tm=512, fc=256):
    T = x2.shape[0]
    row = lambda i: (i, 0)
    consts = (w1, w2, g1, b1, wg, wu, wd, g2, b2)
    return pl.pallas_call(
        functools.partial(_ev_out_ffn_kernel, fc=fc),
        grid=(T // tm,),
        in_specs=[
            pl.BlockSpec((tm, D_MODEL), row),
            pl.BlockSpec((tm, CONV_CH), row),
            pl.BlockSpec((tm, MLA_HEADS * MLA_V), row),
        ] + [_const_spec(c.shape) for c in consts],
        out_specs=pl.BlockSpec((tm, D_MODEL), row),
        out_shape=jax.ShapeDtypeStruct((T, D_MODEL), F32),
        compiler_params=_cparams(("parallel",)),
        name="ev_out_ffn",
    )(x2, u, att, *consts)


def _od_in_kernel(x_ref, w_ref, wkt_ref, o_ref, kt_ref):
    xb = x_ref[...].astype(BF16)
    o_ref[...] = _dot(xb, w_ref[...])
    kt_ref[...] = _dot_nt(wkt_ref[...], xb) * (ML_HEAD_DIM ** -0.5)


def _od_in(x2, w, wkt, seq, tm=512):
    T = x2.shape[0]
    nps = seq // tm
    row = lambda i: (i, 0)
    return pl.pallas_call(
        _od_in_kernel,
        grid=(T // tm,),
        in_specs=[pl.BlockSpec((tm, D_MODEL), row), _const_spec(w.shape), _const_spec(wkt.shape)],
        out_specs=[pl.BlockSpec((tm, OD_COLS), row),
                   pl.BlockSpec((ML_INNER, tm), lambda i: (i // nps, i % nps))],
        out_shape=[jax.ShapeDtypeStruct((T, OD_COLS), F32),
                   jax.ShapeDtypeStruct((T // seq * ML_INNER, seq), F32)],
        compiler_params=_cparams(("parallel",)),
        name="od_in",
    )(x2, w, wkt)


def _tri(reverse):
    i = lax.broadcasted_iota(jnp.int32, (CHUNK, CHUNK), 0)
    j = lax.broadcasted_iota(jnp.int32, (CHUNK, CHUNK), 1)
    mask = (j >= i) if reverse else (j <= i)
    return mask, mask.astype(F32)


def _softplus(v):
    return jnp.maximum(v, 0.0) + jnp.log1p(jnp.exp(-jnp.abs(v)))


def _ssd_body(xbc_ref, sm_ref, bias_ref, a_ref, y_ref, st_ref, *, reverse, lane0):
    mask, tri = _tri(reverse)
    last = 0 if reverse else CHUNK - 1
    dt_all = _softplus(sm_ref[...] + bias_ref[...])
    da_all = dt_all * a_ref[...]
    cs_all = jnp.dot(tri, da_all, preferred_element_type=F32, precision=lax.Precision.HIGHEST)
    cs_t = cs_all.T
    tot_all = cs_all[last:last + 1, :]
    grow_all = jnp.exp(cs_all)
    rest_all = jnp.exp(tot_all - cs_all)
    etot_all = jnp.exp(tot_all)
    P = SSD_HEAD_DIM
    lo = lax.broadcasted_iota(jnp.int32, (CHUNK, 2 * P), 1) < P
    lo_row = lo[0:1, :]
    pairs_per_group = SSD_HEADS // SSD_GROUPS // 2
    prev = [st_ref[p] for p in range(SSD_HEADS // 2)]

    def pick(arr, l0):
        return jnp.where(lo if arr.shape[0] > 1 else lo_row, arr[:, l0:l0 + 1], arr[:, l0 + 1:l0 + 2])

    for g in range(SSD_GROUPS):
        b0 = SSD_INNER + g * SSD_STATE
        c0 = SSD_INNER + SSD_GROUPS * SSD_STATE + g * SSD_STATE
        cm = xbc_ref[:, c0:c0 + SSD_STATE].astype(BF16)
        bm_t = xbc_ref[:, b0:b0 + SSD_STATE].T.astype(BF16)
        cb = _dot(cm, bm_t)
        for pp in range(pairs_per_group):
            p = g * pairs_per_group + pp
            l0 = lane0 + 2 * p
            ps = slice(p * 2 * P, (p + 1) * 2 * P)
            xdt = xbc_ref[:, ps] * pick(dt_all, l0)
            x_lo = jnp.where(lo, xdt, 0.0).astype(BF16)
            x_hi = jnp.where(lo, 0.0, xdt).astype(BF16)
            dec0 = jnp.exp(jnp.where(mask, cs_all[:, l0:l0 + 1] - cs_t[l0:l0 + 1, :], -jnp.inf))
            dec1 = jnp.exp(jnp.where(mask, cs_all[:, l0 + 1:l0 + 2] - cs_t[l0 + 1:l0 + 2, :], -jnp.inf))
            y_diag = _dot((cb * dec0).astype(BF16), x_lo) + _dot((cb * dec1).astype(BF16), x_hi)
            y_off = _dot(cm, prev[p].astype(BF16)) * pick(grow_all, l0)
            y_ref[:, ps] = y_diag + y_off
            xw = (xdt * pick(rest_all, l0)).astype(BF16)
            st_ref[p] = pick(etot_all, l0) * prev[p] + _dot(bm_t, xw)


def _running_max(x, reverse):
    n = x.shape[0]
    row = lax.broadcasted_iota(jnp.int32, x.shape, 0)
    s = 1
    while s < n:
        if reverse:
            sh = jnp.where(row < n - s, pltpu.roll(x, n - s, 0), -jnp.inf)
        else:
            sh = jnp.where(row >= s, pltpu.roll(x, s, 0), -jnp.inf)
        x = jnp.maximum(x, sh)
        s *= 2
    return x


def _mlstm_body(q_ref, kt_ref, v_ref, ig_ref, fg_ref, ib_ref, fb_ref, h_ref, cn_ref, m_ref, *, reverse, direction):
    L = CHUNK
    mask, tri = _tri(reverse)
    last = 0 if reverse else L - 1
    li = ig_ref[...] + ib_ref[...]
    pre = fg_ref[...] + fb_ref[...]
    lf = jnp.minimum(pre, 0.0) - jnp.log1p(jnp.exp(-jnp.abs(pre)))
    bc = jnp.dot(tri, lf, preferred_element_type=F32, precision=lax.Precision.HIGHEST)
    u = li - bc
    cm = _running_max(u, reverse)
    m_prev = m_ref[...]
    big_m = jnp.maximum(m_prev, cm)
    g = bc[last:last + 1, :]
    m_loc = g + cm[last:last + 1, :]
    u_t = u.T
    e_end_t = jnp.exp(g + u - m_loc).T
    m_new = jnp.maximum(g + m_prev, m_loc)
    a_old = jnp.exp(g + m_prev - m_new)
    a_new = jnp.exp(m_loc - m_new)
    w_inter = jnp.exp(m_prev - big_m)
    emt = jnp.exp(-(bc + big_m))
    m_ref[...] = m_new

    lane = lax.broadcasted_iota(jnp.int32, (L, 128), 1)
    lo = lane < ML_HEAD_DIM
    row = lax.broadcasted_iota(jnp.int32, (128, 2 * 128), 0)
    col = lax.broadcasted_iota(jnp.int32, (128, 2 * 128), 1)
    block_diag = (row < ML_HEAD_DIM) == ((col % 128) < ML_HEAD_DIM)
    row_lo = lax.broadcasted_iota(jnp.int32, (128, L), 0) < ML_HEAD_DIM
    ones = jnp.ones((L, 128), F32)
    npairs = ML_HEADS // 2
    prev = [cn_ref[p] for p in range(npairs)]
    for p in range(npairs):
        l0 = direction * ML_HEADS + 2 * p
        l1 = l0 + 1
        ps = slice(p * 128, (p + 1) * 128)
        qp = q_ref[:, ps]
        vp = v_ref[:, ps]
        kt = kt_ref[ps, :]
        ktb = kt.astype(BF16)
        q_lo = jnp.where(lo, qp, 0.0).astype(BF16)
        q_hi = jnp.where(lo, 0.0, qp).astype(BF16)
        vo_lo = jnp.concatenate([jnp.where(lo, vp, 0.0), jnp.where(lo, ones, 0.0)], -1).astype(BF16)
        vo_hi = jnp.concatenate([jnp.where(lo, 0.0, vp), jnp.where(lo, 0.0, ones)], -1).astype(BF16)
        w0 = jnp.where(mask, jnp.exp(u_t[l0:l0 + 1, :] - big_m[:, l0:l0 + 1]), 0.0)
        w1 = jnp.where(mask, jnp.exp(u_t[l1:l1 + 1, :] - big_m[:, l1:l1 + 1]), 0.0)
        a0 = (_dot(q_lo, ktb) * w0).astype(BF16)
        a1 = (_dot(q_hi, ktb) * w1).astype(BF16)
        wi = jnp.where(lo, w_inter[:, l0:l0 + 1], w_inter[:, l1:l1 + 1])
        wi2 = jnp.concatenate([wi, wi], -1)
        nd = _dot(a0, vo_lo) + _dot(a1, vo_hi) + wi2 * _dot(qp.astype(BF16), prev[p].astype(BF16))
        floor = jnp.where(lo, emt[:, l0:l0 + 1], emt[:, l1:l1 + 1])
        h_ref[:, ps] = nd[:, :128] / jnp.maximum(jnp.abs(nd[:, 128:]), floor)
        e_t = jnp.where(row_lo, e_end_t[l0:l0 + 1, :], e_end_t[l1:l1 + 1, :])
        kte = (kt * e_t).astype(BF16)
        vo = jnp.concatenate([vp, ones], -1).astype(BF16)
        s_loc = jnp.where(block_diag, _dot(kte, vo), 0.0)
        row2 = lax.broadcasted_iota(jnp.int32, (128, 1), 0) < ML_HEAD_DIM
        ao = jnp.where(row2, a_old[:, l0:l0 + 1], a_old[:, l1:l1 + 1])
        an = jnp.where(row2, a_new[:, l0:l0 + 1], a_new[:, l1:l1 + 1])
        cn_ref[p] = ao * prev[p] + an * s_loc


def _odd_mix_kernel(xbc_f, dt_f, q_f, kt_f, v_f, ig_f, fg_f, xbc_r, dt_r, q_r, kt_r, v_r, ig_r, fg_r,
                    dtb_f, a_f, dtb_r, a_r, ib_ref, fb_ref, yf_ref, yb_ref, hf_ref, hb_ref,
                    st_f, st_r, cn_f, cn_r, m_f, m_r):
    @pl.when(pl.program_id(1) == 0)
    def _():
        for ref in (st_f, st_r, cn_f, cn_r, m_f, m_r):
            ref[...] = jnp.zeros(ref.shape, F32)

    _ssd_body(xbc_f, dt_f, dtb_f, a_f, yf_ref, st_f, reverse=False, lane0=0)
    _ssd_body(xbc_r, dt_r, dtb_r, a_r, yb_ref, st_r, reverse=True, lane0=SSD_HEADS)
    _mlstm_body(q_f, kt_f, v_f, ig_f, fg_f, ib_ref, fb_ref, hf_ref, cn_f, m_f, reverse=False, direction=0)
    _mlstm_body(q_r, kt_r, v_r, ig_r, fg_r, ib_ref, fb_ref, hb_ref, cn_r, m_r, reverse=True, direction=1)


def _odd_mix(xbc, proj, kt, S, w):
    T = proj.shape[0]
    B = T // S
    nc = S // CHUNK
    fwd = lambda c: c
    rev = lambda c: nc - 1 - c

    def specs(cidx):
        col = lambda cb: (lambda b, c: (b * nc + cidx(c), cb))
        return [
            pl.BlockSpec((CHUNK, SSD_XBC), col(0)),
            pl.BlockSpec((CHUNK, 128), col(OD_DT // 128)),
            pl.BlockSpec((CHUNK, ML_INNER), col(OD_Q // 512)),
            pl.BlockSpec((ML_INNER, CHUNK), lambda b, c: (b, cidx(c))),
            pl.BlockSpec((CHUNK, ML_INNER), col(OD_V // 512)),
            pl.BlockSpec((CHUNK, 128), col(OD_IG // 128)),
            pl.BlockSpec((CHUNK, 128), col(OD_FG // 128)),
        ]

    rows = [w["dt_bias"][0], w["a"][0], w["dt_bias"][1], w["a"][1], w["ig_b"], w["fg_b"]]
    out = lambda cidx: pl.BlockSpec((CHUNK, 512), lambda b, c: (b * nc + cidx(c), 0))
    seq = (xbc, proj, proj, kt, proj, proj, proj)
    return pl.pallas_call(
        _odd_mix_kernel,
        grid=(B, nc),
        in_specs=specs(fwd) + specs(rev) + [_const_spec(r.shape) for r in rows],
        out_specs=[out(fwd), out(rev), out(fwd), out(rev)],
        out_shape=[jax.ShapeDtypeStruct((T, 512), F32)] * 4,
        scratch_shapes=[
            pltpu.VMEM((SSD_HEADS // 2, SSD_STATE, 2 * SSD_HEAD_DIM), F32),
            pltpu.VMEM((SSD_HEADS // 2, SSD_STATE, 2 * SSD_HEAD_DIM), F32),
            pltpu.VMEM((ML_HEADS // 2, 128, 256), F32),
            pltpu.VMEM((ML_HEADS // 2, 128, 256), F32),
            pltpu.VMEM((1, 128), F32),
            pltpu.VMEM((1, 128), F32),
        ],
        compiler_params=_cparams(("parallel", "arbitrary")),
        name="odd_mix",
    )(*seq, *seq, *rows)


def _od_out_kernel(x_ref, z_ref, xs_ref, o_ref, yf_ref, yb_ref, hf_ref, hb_ref, dsk_ref, sg_ref, mg_ref, avg_ref,
                   w1_ref, w2_ref, g_ref, b_ref, rw_ref, rb_ref, x1_ref, route_ref, route_t_ref):
    y = (yf_ref[...] + yb_ref[...] + xs_ref[...] * dsk_ref[...]) * _silu(z_ref[...])
    gw = SSD_INNER // SSD_GROUPS
    m = None
    for g in range(SSD_GROUPS):
        sl = slice(g * gw, (g + 1) * gw)
        yn = _rmsnorm(y[:, sl], sg_ref[:, sl]).astype(BF16)
        t = _dot(yn, w1_ref[sl, :])
        m = t if m is None else m + t
    hs = hf_ref[...] + hb_ref[...]
    avg = avg_ref[...]

    def head_mean(v):
        hi = v.astype(BF16)
        lo = (v - hi.astype(F32)).astype(BF16)
        return _dot(hi, avg) + _dot(lo, avg)

    dv = hs - head_mean(hs)
    var = head_mean(dv * dv)
    hn = jax.nn.sigmoid(o_ref[...]) * (dv * lax.rsqrt(var + LN_EPS) * mg_ref[...])
    m = m + _dot(hn.astype(BF16), w2_ref[...])
    x1 = _layernorm(ALPHA * x_ref[...] + m, g_ref[...], b_ref[...])
    x1_ref[...] = x1
    logits = _dot(x1.astype(BF16), rw_ref[...]) + rb_ref[...]
    lane = lax.broadcasted_iota(jnp.int32, logits.shape, 1).astype(F32)
    m1 = jnp.max(logits, -1, keepdims=True)
    i1 = jnp.min(jnp.where(logits == m1, lane, 128.0), -1, keepdims=True)
    rest = jnp.where(lane == i1, -jnp.inf, logits)
    m2 = jnp.max(rest, -1, keepdims=True)
    i2 = jnp.min(jnp.where(rest == m2, lane, 128.0), -1, keepdims=True)
    e = jnp.exp(m2 - m1)
    g1 = 1.0 / (1.0 + e)
    g2 = e / (1.0 + e)
    route = jnp.where(lane == 0.0, i1,
                      jnp.where(lane == 1.0, i2, jnp.where(lane == 2.0, g1, jnp.where(lane == 3.0, g2, 0.0))))
    route_ref[...] = route
    route_t_ref[...] = route.T[0:8, :]


def _od_out(x2, proj, xbc, yf, yb, hf, hb, dsk, sg, mg, w1, w2, g, b, rw, rb, tm=256):
    T = x2.shape[0]
    row = lambda i: (i, 0)
    colb = lambda cb: (lambda i: (i, cb))
    head = jnp.arange(ML_INNER) // ML_HEAD_DIM
    avg = jnp.where(head[:, None] == head[None, :], 1.0 / ML_HEAD_DIM, 0.0).astype(BF16)
    return pl.pallas_call(
        _od_out_kernel,
        grid=(T // tm,),
        in_specs=[
            pl.BlockSpec((tm, D_MODEL), row),
            pl.BlockSpec((tm, 512), colb(OD_Z // 512)),
            pl.BlockSpec((tm, 512), colb(0)),
            pl.BlockSpec((tm, 512), colb(OD_O // 512)),
            pl.BlockSpec((tm, 512), row), pl.BlockSpec((tm, 512), row),
            pl.BlockSpec((tm, 512), row), pl.BlockSpec((tm, 512), row),
            _const_spec(dsk.shape), _const_spec(sg.shape), _const_spec(mg.shape), _const_spec(avg.shape),
            _const_spec(w1.shape), _const_spec(w2.shape), _const_spec(g.shape), _const_spec(b.shape),
            _const_spec(rw.shape), _const_spec(rb.shape),
        ],
        out_specs=[pl.BlockSpec((tm, D_MODEL), row), pl.BlockSpec((tm, 128), row),
                   pl.BlockSpec((8, tm), lambda i: (0, i))],
        out_shape=[jax.ShapeDtypeStruct((T, D_MODEL), F32), jax.ShapeDtypeStruct((T, 128), F32),
                   jax.ShapeDtypeStruct((8, T), F32)],
        compiler_params=_cparams(("parallel",)),
        name="od_out",
    )(x2, proj, xbc, proj, yf, yb, hf, hb, dsk, sg, mg, avg, w1, w2, g, b, rw, rb)


def _row_copy(src_hbm, idx, dst, r, sem):
    return pltpu.make_async_copy(src_hbm.at[pl.ds(idx, 1)], dst.at[pl.ds(r, 1)], sem)


def _gather_rows(src_hbm, idx_smem, dst, sem, n):
    for r in range(n):
        _row_copy(src_hbm, idx_smem[0, r], dst, r, sem).start(priority=r % 2)


def _gather_wait(src_hbm, dst, sem):
    pltpu.make_async_copy(src_hbm.at[pl.ds(0, dst.shape[0])], dst, sem).wait()


def _gather_loop(src_hbm, idx_smem, idx0, dst, sem):
    def start(r, carry):
        _row_copy(src_hbm, idx_smem[0, idx0 + r], dst, r, sem).start()
        return carry

    lax.fori_loop(0, dst.shape[0], start, 0, unroll=8)


def _moe_kernel(te_ref, tok0_ref, tokn_ref, x_hbm, wg_ref, wu_ref, wd_ref, o_ref,
                xbuf, xb_ref, acc_ref, sem, *, nfc, sub):
    i = pl.program_id(0)
    j = pl.program_id(1)
    n = pl.num_programs(0)
    per_step = xbuf.shape[1]

    @pl.when(jnp.logical_and(i == 0, j == 0))
    def _():
        for jj in range(nfc):
            _gather_loop(x_hbm, tok0_ref, jj * per_step, xbuf.at[jj], sem)

    @pl.when(j == 0)
    def _():
        for jj in range(nfc):
            _gather_wait(x_hbm, xbuf.at[jj], sem)
        for jj in range(nfc):
            xb_ref[jj * per_step:(jj + 1) * per_step, :] = xbuf[jj].astype(BF16)
        acc_ref[...] = jnp.zeros(acc_ref.shape, F32)

    xb = xb_ref[...]
    acc = acc_ref[...]
    for c in range(wg_ref.shape[1] // sub):
        sl = slice(c * sub, (c + 1) * sub)
        hh = _silu(_dot(xb, wg_ref[:, sl])) * _dot(xb, wu_ref[:, sl])
        acc = acc + _dot(hh.astype(BF16), wd_ref[sl, :])
    acc_ref[...] = acc
    _gather_rows(x_hbm, tokn_ref, xbuf.at[j], sem, per_step)

    @pl.when(j == nfc - 1)
    def _():
        o_ref[...] = acc_ref[...]

    @pl.when(jnp.logical_and(i == n - 1, j == nfc - 1))
    def _():
        for jj in range(nfc):
            _gather_wait(x_hbm, xbuf.at[jj], sem)


def _moe(x1, tile_expert, row_token, wg, wu, wd, tm, fc=1792, sub=256):
    n_tiles = tile_expert.shape[0]
    nfc = D_FF_EXPERT // fc
    per_step = tm // nfc
    grid_spec = pltpu.PrefetchScalarGridSpec(
        num_scalar_prefetch=1,
        grid=(n_tiles, nfc),
        in_specs=[
            pl.BlockSpec((None, 1, tm), lambda i, j, te: (0, 0, 0), memory_space=pltpu.SMEM),
            pl.BlockSpec((None, 1, per_step), lambda i, j, te: ((i + 1) * nfc + j, 0, 0), memory_space=pltpu.SMEM),
            pl.BlockSpec(memory_space=pl.ANY),
            pl.BlockSpec((None, D_MODEL, fc), lambda i, j, te: (te[i], 0, j)),
            pl.BlockSpec((None, D_MODEL, fc), lambda i, j, te: (te[i], 0, j)),
            pl.BlockSpec((None, fc, D_MODEL), lambda i, j, te: (te[i], j, 0)),
        ],
        out_specs=pl.BlockSpec((tm, D_MODEL), lambda i, j, te: (i, 0)),
        scratch_shapes=[
            pltpu.VMEM((nfc, per_step, D_MODEL), F32),
            pltpu.VMEM((tm, D_MODEL), BF16),
            pltpu.VMEM((tm, D_MODEL), F32),
            pltpu.SemaphoreType.DMA,
        ],
    )
    return pl.pallas_call(
        functools.partial(_moe_kernel, nfc=nfc, sub=sub),
        grid_spec=grid_spec,
        out_shape=jax.ShapeDtypeStruct((n_tiles * tm, D_MODEL), F32),
        compiler_params=_cparams(("arbitrary", "arbitrary")),
        name="moe_experts",
    )(tile_expert, row_token.reshape(n_tiles + 1, 1, tm), row_token.reshape((n_tiles + 1) * nfc, 1, per_step),
      x1, wg, wu, wd)


def _combine_kernel(pos0_ref, posa_ref, posb_ref, x_ref, route_ref, y_hbm, g_ref, b_ref, o_ref, ybuf, sem, *, tm):
    s = pl.program_id(0)
    n = pl.num_programs(0)

    @pl.when(s == 0)
    def _():
        for half in range(2):
            _gather_loop(y_hbm, pos0_ref, half * 2 * tm, ybuf.at[half], sem.at[half])

    for half, pos_ref in ((0, posa_ref), (1, posb_ref)):
        _gather_wait(y_hbm, ybuf.at[half], sem.at[half])
        rows = slice(half * tm, (half + 1) * tm)
        g0 = route_ref[rows, 2:3]
        g1 = route_ref[rows, 3:4]
        f = g0 * ybuf[half, 0:tm, :] + g1 * ybuf[half, tm:2 * tm, :]
        _gather_rows(y_hbm, pos_ref, ybuf.at[half], sem.at[half], 2 * tm)
        o_ref[rows, :] = _layernorm(ALPHA * x_ref[rows, :] + f, g_ref[...], b_ref[...])

    @pl.when(s == n - 1)
    def _():
        _gather_wait(y_hbm, ybuf.at[0], sem.at[0])
        _gather_wait(y_hbm, ybuf.at[1], sem.at[1])


def _combine(x1, route, y_sorted, pos, g, b, tm=256):
    T = route.shape[0]
    nt = T // tm
    assert nt % 2 == 0
    return pl.pallas_call(
        functools.partial(_combine_kernel, tm=tm),
        grid=(nt // 2,),
        in_specs=[
            pl.BlockSpec((None, 1, 4 * tm), lambda s: (0, 0, 0), memory_space=pltpu.SMEM),
            pl.BlockSpec((None, 1, 2 * tm), lambda s: (2 * s + 2, 0, 0), memory_space=pltpu.SMEM),
            pl.BlockSpec((None, 1, 2 * tm), lambda s: (2 * s + 3, 0, 0), memory_space=pltpu.SMEM),
            pl.BlockSpec((2 * tm, D_MODEL), lambda s: (s, 0)),
            pl.BlockSpec((2 * tm, 128), lambda s: (s, 0)),
            pl.BlockSpec(memory_space=pl.ANY),
            _const_spec(g.shape), _const_spec(b.shape),
        ],
        out_specs=pl.BlockSpec((2 * tm, D_MODEL), lambda s: (s, 0)),
        out_shape=jax.ShapeDtypeStruct((T, D_MODEL), F32),
        scratch_shapes=[pltpu.VMEM((2, 2 * tm, D_MODEL), F32), pltpu.SemaphoreType.DMA((2,))],
        compiler_params=_cparams(("arbitrary",)),
        name="moe_combine",
    )(pos.reshape((nt + 2) // 2, 1, 4 * tm), pos, pos, x1, route, y_sorted, g, b)


def _route_tables(route_t, tm_e, tm_c):
    T = route_t.shape[1]
    A = 2 * T
    e0 = route_t[0].astype(jnp.int32)
    e1 = route_t[1].astype(jnp.int32)
    ids = jnp.arange(N_EXPERTS, dtype=jnp.int32)[:, None]
    oh0 = e0[None, :] == ids
    oh1 = e1[None, :] == ids
    c0 = jnp.cumsum(oh0.astype(jnp.int32), axis=1)
    c1 = jnp.cumsum(oh1.astype(jnp.int32), axis=1)
    n0 = c0[:, -1:]
    counts = (n0 + c1[:, -1:])[:, 0]
    start = jnp.cumsum(counts) - counts
    padded = ((counts + tm_e - 1) // tm_e) * tm_e
    pend = jnp.cumsum(padded)
    pstart = pend - padded
    dest0 = jnp.sum(jnp.where(oh0, pstart[:, None] + c0 - 1, 0), axis=0)
    dest1 = jnp.sum(jnp.where(oh1, pstart[:, None] + n0 + c1 - 1, 0), axis=0)
    order = jnp.argsort(jnp.concatenate([e0, e1]), stable=True).astype(jnp.int32)
    n_tiles = A // tm_e + N_EXPERTS
    rows = jnp.arange(n_tiles * tm_e, dtype=jnp.int32)
    past = rows[None, :] >= pend[:, None]
    e_row = jnp.minimum(jnp.sum(past.astype(jnp.int32), axis=0), N_EXPERTS - 1)
    local = rows - jnp.sum(jnp.where(past, padded[:, None], 0), axis=0)
    count_row = jnp.sum(jnp.where(e_row[None, :] == ids, counts[:, None], 0), axis=0)
    ok = jnp.logical_and(local < count_row, rows < pend[-1])
    src = jnp.sum(jnp.where(past, counts[:, None], 0), axis=0) + local
    a_row = order[jnp.clip(src, 0, A - 1)]
    row_token = jnp.where(ok, jnp.where(a_row >= T, a_row - T, a_row), 0).astype(jnp.int32)
    row_token = jnp.concatenate([row_token, jnp.zeros((tm_e,), jnp.int32)])
    tile_expert = e_row[::tm_e]
    nt = T // tm_c
    pos = jnp.concatenate([dest0.reshape(nt, tm_c), dest1.reshape(nt, tm_c)], axis=1)
    pos = jnp.concatenate([pos, jnp.zeros((2, 2 * tm_c), jnp.int32)], 0).astype(jnp.int32)
    return tile_expert, row_token, pos.reshape(nt + 2, 1, 2 * tm_c)


def _rot_cols(w):
    half = MLA_ROPE // 2
    return jnp.concatenate([-w[..., half:], w[..., :half]], -1)


def _prep_even(p, j):
    w_in = p["ev_w_in"][j]
    c_rot = 2 * CONV_CH + MLA_Q_LORA + MLA_KV_LORA
    k_rot = w_in[:, c_rot:c_rot + MLA_ROPE]
    z64 = jnp.zeros((D_MODEL, MLA_NOPE), F32)
    z32 = jnp.zeros((D_MODEL, HEAD_PAD - MLA_NOPE - MLA_ROPE), F32)
    w_in2 = jnp.concatenate([w_in[:, :c_rot], z64, k_rot, z32, z64, _rot_cols(k_rot), z32], -1).astype(BF16)
    wq = p["mla_w_uq"][j].reshape(MLA_Q_LORA, MLA_HEADS, MLA_NOPE + MLA_ROPE)
    zq = jnp.zeros((MLA_Q_LORA, MLA_HEADS, HEAD_PAD - MLA_NOPE - MLA_ROPE), F32)
    zq64 = jnp.zeros((MLA_Q_LORA, MLA_HEADS, MLA_NOPE), F32)
    wq_plain = jnp.concatenate([wq, zq], -1).reshape(MLA_Q_LORA, -1)
    wq_rot = jnp.concatenate([zq64, _rot_cols(wq[..., MLA_NOPE:]), zq], -1).reshape(MLA_Q_LORA, -1)
    wq2 = jnp.concatenate([wq_plain, wq_rot], -1).astype(BF16)
    wkv = p["mla_w_ukv"][j].reshape(MLA_KV_LORA, MLA_HEADS, MLA_NOPE + MLA_V)
    zk = jnp.zeros((MLA_KV_LORA, MLA_HEADS, MLA_V), F32)
    wk = jnp.concatenate([wkv[..., :MLA_NOPE], zk], -1).reshape(MLA_KV_LORA, -1)
    wv = wkv[..., MLA_NOPE:]
    even = (jnp.arange(MLA_HEADS) % 2 == 0)[None, :, None]
    wv2 = jnp.concatenate([jnp.where(even, wv, 0.0), jnp.where(even, 0.0, wv)], -1).reshape(MLA_KV_LORA, -1)
    wkv2 = jnp.concatenate([wk, wv2], -1).astype(BF16)
    w_out = p["ev_w_out"][j].astype(BF16)
    return dict(
        w_in=w_in2, qg=p["mla_q_norm_g"][j][None], wq=wq2, kvg=p["mla_kv_norm_g"][j][None], wkv=wkv2,
        dw_w=p["conv_dw_w"][j], dw_b=p["conv_dw_b"][j][None], cln_g=p["conv_ln_g"][j][None],
        cln_b=p["conv_ln_b"][j][None], w1=w_out[:CONV_CH], w2=w_out[CONV_CH:],
        wg=p["ffn_w_gate"][j].astype(BF16), wu=p["ffn_w_up"][j].astype(BF16), wd=p["ffn_w_down"][j].astype(BF16),
    )


def _lane_row(vals, lane0):
    return jnp.zeros((1, 128), F32).at[0, lane0:lane0 + vals.shape[0]].set(vals)


def _prep_odd(p, j):
    w = p["od_w_in"][j]
    z112 = jnp.zeros((D_MODEL, 128 - 16), F32)
    w_in2 = jnp.concatenate([w[:, 0:1536], w[:, 1552:2064], w[:, 2576:3600],
                             w[:, 1536:1552], z112, w[:, 3600:3616], z112, w[:, 3616:3632], z112], -1).astype(BF16)
    wkt = w[:, 2064:2576].T.astype(BF16)
    w_out = p["od_w_out"][j].astype(BF16)
    rw = jnp.concatenate([p["moe_router_w"][j], jnp.zeros((D_MODEL, 128 - N_EXPERTS), F32)], -1).astype(BF16)
    rb = jnp.full((1, 128), -jnp.inf, F32).at[0, :N_EXPERTS].set(p["moe_router_b"][j])
    a = -jnp.exp(p["ssd_a_log"][j])
    return dict(
        w_in=w_in2, wkt=wkt, cw=p["ssd_conv_w"][j], cb=p["ssd_conv_b"][j][None],
        dt_bias=[_lane_row(p["ssd_dt_bias"][j][d], d * SSD_HEADS) for d in range(2)],
        a=[_lane_row(a[d], d * SSD_HEADS) for d in range(2)],
        ig_b=_lane_row(p["ml_igate_b"][j].reshape(-1), 0), fg_b=_lane_row(p["ml_fgate_b"][j].reshape(-1), 0),
        dsk=jnp.repeat(p["ssd_d"][j], SSD_HEAD_DIM)[None], sg=p["ssd_norm_g"][j][None], mg=p["ml_norm_g"][j][None],
        w1=w_out[:SSD_INNER], w2=w_out[SSD_INNER:], rw=rw, rb=rb,
        wg=p["moe_w_gate"][j].astype(BF16), wu=p["moe_w_up"][j].astype(BF16), wd=p["moe_w_down"][j].astype(BF16),
    )


def _rope_tables(seq):
    half = MLA_ROPE // 2
    inv_freq = ROPE_THETA ** (-jnp.arange(half, dtype=F32) / half)
    ang = jnp.arange(seq, dtype=F32)[:, None] * inv_freq
    cos2 = jnp.concatenate([jnp.cos(ang), jnp.cos(ang)], -1)
    sin2 = jnp.concatenate([jnp.sin(ang), jnp.sin(ang)], -1)
    pad = jnp.zeros((seq, HEAD_PAD - MLA_NOPE - MLA_ROPE), F32)
    cos_t = jnp.concatenate([jnp.ones((seq, MLA_NOPE), F32), cos2, pad], -1)
    sin_t = jnp.concatenate([jnp.zeros((seq, MLA_NOPE), F32), sin2, pad], -1)
    return cos_t, sin_t


def _even_layer(x2, B, S, w, ln, cos_t, sin_t):
    tm = min(512, S)
    u, q, k, v = _ev_in(x2, cos_t, sin_t, w["w_in"], w["qg"], w["wq"], w["kvg"], w["wkv"], S, tm=tm)
    uc = _dwconv(u, S, 0, w["dw_w"], w["dw_b"], w["cln_g"], w["cln_b"], with_ln=True, out_dtype=BF16)
    att = _attention(q, k, v, B, S, tq=min(256, S))
    return _ev_out_ffn(x2, uc, att, w["w1"], w["w2"], ln[0], ln[1], w["wg"], w["wu"], w["wd"], ln[2], ln[3], tm=tm)


def _odd_layer(x2, B, S, w, ln, tm_e=512, tm_c=256):
    proj, kt = _od_in(x2, w["w_in"], w["wkt"], S, tm=min(512, S))
    zeros = jnp.zeros((1, SSD_XBC), F32)
    xbc = _dwconv(proj, S, OD_XBC // 512, w["cw"], w["cb"], zeros, zeros, with_ln=False, out_dtype=F32, ncb=2)
    yf, yb, hf, hb = _odd_mix(xbc, proj, kt, S, w)
    x1, route, route_t = _od_out(x2, proj, xbc, yf, yb, hf, hb, w["dsk"], w["sg"], w["mg"],
                                 w["w1"], w["w2"], ln[0], ln[1], w["rw"], w["rb"])
    te, row_token, pos = _route_tables(route_t, tm_e, tm_c)
    y_sorted = _moe(x1, te, row_token, w["wg"], w["wu"], w["wd"], tm_e)
    return _combine(x1, route, y_sorted, pos, ln[2], ln[3], tm_c)


def _trunk(x, p):
    B, S, _ = x.shape
    x2 = x.reshape(B * S, D_MODEL)
    cos_t, sin_t = _rope_tables(S)
    for l in range(DEPTH):
        j = l // 2
        ln = (p["ln1_g"][l][None], p["ln1_b"][l][None], p["ln2_g"][l][None], p["ln2_b"][l][None])
        if l % 2 == 0:
            x2 = _even_layer(x2, B, S, _prep_even(p, j), ln, cos_t, sin_t)
        else:
            x2 = _odd_layer(x2, B, S, _prep_odd(p, j), ln)
    return x2.reshape(B, S, D_MODEL)


def kernel(x_prompt, x_sample, ev_w_in, conv_dw_w, conv_dw_b, conv_ln_g, conv_ln_b, mla_q_norm_g, mla_w_uq, mla_kv_norm_g, mla_w_ukv, ev_w_out, od_w_in, ssd_conv_w, ssd_conv_b, ssd_dt_bias, ssd_a_log, ssd_d, ssd_norm_g, ml_igate_b, ml_fgate_b, ml_norm_g, od_w_out, ffn_w_gate, ffn_w_up, ffn_w_down, moe_router_w, moe_router_b, moe_w_gate, moe_w_up, moe_w_down, ln1_g, ln1_b, ln2_g, ln2_b):
    p = dict(ev_w_in=ev_w_in, conv_dw_w=conv_dw_w, conv_dw_b=conv_dw_b, conv_ln_g=conv_ln_g, conv_ln_b=conv_ln_b,
             mla_q_norm_g=mla_q_norm_g, mla_w_uq=mla_w_uq, mla_kv_norm_g=mla_kv_norm_g, mla_w_ukv=mla_w_ukv,
             ev_w_out=ev_w_out, od_w_in=od_w_in, ssd_conv_w=ssd_conv_w, ssd_conv_b=ssd_conv_b,
             ssd_dt_bias=ssd_dt_bias, ssd_a_log=ssd_a_log, ssd_d=ssd_d, ssd_norm_g=ssd_norm_g,
             ml_igate_b=ml_igate_b, ml_fgate_b=ml_fgate_b, ml_norm_g=ml_norm_g, od_w_out=od_w_out,
             ffn_w_gate=ffn_w_gate, ffn_w_up=ffn_w_up, ffn_w_down=ffn_w_down, moe_router_w=moe_router_w,
             moe_router_b=moe_router_b, moe_w_gate=moe_w_gate, moe_w_up=moe_w_up, moe_w_down=moe_w_down,
             ln1_g=ln1_g, ln1_b=ln1_b, ln2_g=ln2_g, ln2_b=ln2_b)
    assert x_prompt.shape[1] == x_sample.shape[1]
    nb = x_prompt.shape[0]
    y = _trunk(jnp.concatenate([x_prompt, x_sample], 0), p)
    return (y[:nb], y[nb:])
```

```python
import functools
import math

import jax
import jax.numpy as jnp
import numpy as np
from jax import lax
from jax.experimental import pallas as pl
from jax.experimental.pallas import tpu as pltpu

F32 = jnp.float32
BF16 = jnp.bfloat16

D_MODEL = 1024
DEPTH = 4
ALPHA = (2.0 * DEPTH) ** 0.25
LN_EPS = 1e-5
RMS_EPS = 1e-6

CONV_CH = 512
CONV_W = 31
MLA_HEADS = 8
MLA_NOPE = 64
MLA_ROPE = 32
MLA_V = 64
MLA_Q_LORA = 256
MLA_KV_LORA = 128
ROPE_THETA = 10000.0
HEAD_PAD = 128
EV_COLS = 2 * CONV_CH + MLA_Q_LORA + MLA_KV_LORA + 2 * HEAD_PAD
Q_SCALE = (MLA_NOPE + MLA_ROPE) ** -0.5 * math.log2(math.e)

SSD_HEADS = 8
SSD_HEAD_DIM = 64
SSD_INNER = 512
SSD_GROUPS = 2
SSD_STATE = 128
SSD_CONV_W = 5
SSD_XBC = 1024
CHUNK = 128
ML_HEADS = 8
ML_HEAD_DIM = 64
ML_INNER = 512
OD_Z, OD_XBC, OD_Q, OD_V, OD_O, OD_DT, OD_IG, OD_FG = 0, 512, 1536, 2048, 2560, 3072, 3200, 3328
OD_COLS = 3456

D_FF = 2816
N_EXPERTS = 8
D_FF_EXPERT = 3584

VMEM_LIMIT = 56 * 1024 * 1024


def _cparams(sem):
    return pltpu.CompilerParams(dimension_semantics=sem, vmem_limit_bytes=VMEM_LIMIT)


def _const_spec(shape):
    nd = len(shape)
    return pl.BlockSpec(shape, lambda *_: (0,) * nd, pipeline_mode=pl.Buffered(1))


def _layernorm(v, g, b):
    mu = jnp.mean(v, -1, keepdims=True)
    d = v - mu
    var = jnp.mean(d * d, -1, keepdims=True)
    return d * lax.rsqrt(var + LN_EPS) * g + b


def _rmsnorm(v, g):
    return v * lax.rsqrt(jnp.mean(v * v, -1, keepdims=True) + RMS_EPS) * g


def _silu(v):
    return v * jax.nn.sigmoid(v)


def _dot(a, b):
    return jnp.dot(a, b, preferred_element_type=F32)


def _dot_nt(a, b):
    return lax.dot_general(a, b, (((1,), (1,)), ((), ())), preferred_element_type=F32)


def _ev_in_kernel(x_ref, cos_ref, sin_ref, w_in_ref, qg_ref, wq_ref, kvg_ref, wkv_ref,
                  u_ref, q_ref, k_ref, v_ref):
    xb = x_ref[...].astype(BF16)
    h = _dot(xb, w_in_ref[...])
    u_ref[...] = h[:, :CONV_CH] * jax.nn.sigmoid(h[:, CONV_CH:2 * CONV_CH])
    c0 = 2 * CONV_CH
    cos = cos_ref[...]
    sin = sin_ref[...]
    ql = _rmsnorm(h[:, c0:c0 + MLA_Q_LORA], qg_ref[...]).astype(BF16)
    qq = _dot(ql, wq_ref[...])
    c1 = c0 + MLA_Q_LORA
    kvl = _rmsnorm(h[:, c1:c1 + MLA_KV_LORA], kvg_ref[...]).astype(BF16)
    kk = _dot(kvl, wkv_ref[...])
    c2 = c1 + MLA_KV_LORA
    kpe = h[:, c2:c2 + HEAD_PAD] * cos + h[:, c2 + HEAD_PAD:c2 + 2 * HEAD_PAD] * sin
    nh = MLA_HEADS * HEAD_PAD
    for hd in range(MLA_HEADS):
        sl = slice(hd * HEAD_PAD, (hd + 1) * HEAD_PAD)
        sl2 = slice(nh + hd * HEAD_PAD, nh + (hd + 1) * HEAD_PAD)
        q_ref[:, sl] = ((qq[:, sl] * cos + qq[:, sl2] * sin) * Q_SCALE).astype(BF16)
        k_ref[:, sl] = (kk[:, sl] + kpe).astype(BF16)
    v_ref[...] = kk[:, nh:].astype(BF16)


def _ev_in(x2, cos_t, sin_t, w_in, qg, wq, kvg, wkv, seq, tm=512):
    T = x2.shape[0]
    nps = seq // tm
    row = lambda i: (i, 0)
    pos = lambda i: (i % nps, 0)
    nh = MLA_HEADS * HEAD_PAD
    return pl.pallas_call(
        _ev_in_kernel,
        grid=(T // tm,),
        in_specs=[
            pl.BlockSpec((tm, D_MODEL), row),
            pl.BlockSpec((tm, HEAD_PAD), pos),
            pl.BlockSpec((tm, HEAD_PAD), pos),
            _const_spec(w_in.shape), _const_spec(qg.shape), _const_spec(wq.shape),
            _const_spec(kvg.shape), _const_spec(wkv.shape),
        ],
        out_specs=[
            pl.BlockSpec((tm, CONV_CH), row),
            pl.BlockSpec((tm, nh), row),
            pl.BlockSpec((tm, nh), row),
            pl.BlockSpec((tm, nh), row),
        ],
        out_shape=[
            jax.ShapeDtypeStruct((T, CONV_CH), F32),
            jax.ShapeDtypeStruct((T, nh), BF16),
            jax.ShapeDtypeStruct((T, nh), BF16),
            jax.ShapeDtypeStruct((T, nh), BF16),
        ],
        compiler_params=_cparams(("parallel",)),
        name="ev_in",
    )(x2, cos_t, sin_t, w_in, qg, wq, kvg, wkv)


def _dwconv_kernel(x_ref, w_ref, b_ref, g_ref, beta_ref, o_ref, pad_ref, tmp_ref, *, width, halo, rows, with_ln):
    S, C = x_ref.shape
    half = width // 2
    win = rows + 2 * halo
    pad_ref[0:halo, :] = jnp.zeros((halo, C), F32)
    pad_ref[halo + S:halo + S + halo, :] = jnp.zeros((halo, C), F32)
    pad_ref[halo:halo + S, :] = x_ref[...]

    def tile(t, carry):
        r0 = pl.multiple_of(t * rows, rows)
        for cb in range(C // 128):
            cs = slice(cb * 128, (cb + 1) * 128)
            window = pad_ref[pl.ds(r0, win), cs]
            acc = jnp.zeros((rows, 128), F32) + b_ref[:, cs]
            for r in range(8):
                taps = [w for w in range(width) if (halo + w - half) % 8 == r]
                if taps:
                    rolled = window if r == 0 else pltpu.roll(window, win - r, 0)
                    for w in taps:
                        a0 = halo + w - half - r
                        acc = acc + rolled[a0:a0 + rows] * w_ref[w:w + 1, cs]
            tmp_ref[:, cs] = acc
        acc = tmp_ref[...]
        if with_ln:
            acc = _layernorm(acc, g_ref[...], beta_ref[...])
        o_ref[pl.ds(r0, rows), :] = _silu(acc).astype(o_ref.dtype)
        return carry

    lax.fori_loop(0, S // rows, tile, 0)


def _dwconv(x2, S, col_block, w, b, g, beta, *, with_ln, out_dtype, ncb=1, rows=128):
    B = x2.shape[0] // S
    C = 512
    width = w.shape[0]
    halo = 16
    assert width // 2 <= halo
    kern = functools.partial(_dwconv_kernel, width=width, halo=halo, rows=rows, with_ln=with_ln)
    return pl.pallas_call(
        kern,
        grid=(B, ncb),
        in_specs=[
            pl.BlockSpec((S, C), lambda b, c: (b, col_block + c)),
            pl.BlockSpec((width, C), lambda b, c: (0, c)),
            pl.BlockSpec((1, C), lambda b, c: (0, c)),
            pl.BlockSpec((1, C), lambda b, c: (0, c)),
            pl.BlockSpec((1, C), lambda b, c: (0, c)),
        ],
        out_specs=pl.BlockSpec((S, C), lambda b, c: (b, c)),
        out_shape=jax.ShapeDtypeStruct((B * S, C * ncb), out_dtype),
        scratch_shapes=[pltpu.VMEM((S + 2 * halo, C), F32), pltpu.VMEM((rows, C), F32)],
        compiler_params=_cparams(("parallel", "parallel")),
        name="dwconv_ln" if with_ln else "dwconv",
    )(x2, w, b, g, beta)


ATT_HEADS_PER_STEP = 8


def _attn_kernel(q_ref, k_ref, v_ref, o_ref):
    for pair in range(ATT_HEADS_PER_STEP // 2):
        acc = None
        for j in range(2):
            hd = 2 * pair + j
            sl = slice(hd * HEAD_PAD, (hd + 1) * HEAD_PAD)
            s = _dot_nt(q_ref[:, sl], k_ref[:, sl])
            m = jnp.max(s, -1, keepdims=True)
            p = jnp.exp2(s - m)
            l = jnp.sum(p, -1, keepdims=True)
            o = _dot(p.astype(BF16), v_ref[:, sl]) / l
            acc = o if acc is None else acc + o
        o_ref[:, pair * 2 * MLA_V:(pair + 1) * 2 * MLA_V] = acc.astype(o_ref.dtype)


def _attention(q, k, v, B, S, tq=256):
    nq = S // tq
    T = B * S
    hps = ATT_HEADS_PER_STEP
    return pl.pallas_call(
        _attn_kernel,
        grid=(B, MLA_HEADS // hps, nq),
        in_specs=[
            pl.BlockSpec((tq, hps * HEAD_PAD), lambda b, hp, i: (b * nq + i, hp)),
            pl.BlockSpec((S, hps * HEAD_PAD), lambda b, hp, i: (b, hp), pipeline_mode=pl.Buffered(1)),
            pl.BlockSpec((S, hps * HEAD_PAD), lambda b, hp, i: (b, hp), pipeline_mode=pl.Buffered(1)),
        ],
        out_specs=pl.BlockSpec((tq, hps * MLA_V), lambda b, hp, i: (b * nq + i, hp)),
        out_shape=jax.ShapeDtypeStruct((T, MLA_HEADS * MLA_V), BF16),
        compiler_params=_cparams(("parallel", "parallel", "parallel")),
        name="attention",
    )(q, k, v)


def _ev_out_ffn_kernel(x_ref, u_ref, a_ref, w1_ref, w2_ref, g1_ref, b1_ref, wg_ref, wu_ref, wd_ref, g2_ref, b2_ref,
                       o_ref, *, fc):
    m = _dot(u_ref[...], w1_ref[...]) + _dot(a_ref[...], w2_ref[...])
    x = _layernorm(ALPHA * x_ref[...] + m, g1_ref[...], b1_ref[...])
    xb = x.astype(BF16)
    acc = jnp.zeros(x.shape, F32)
    for c in range(wg_ref.shape[1] // fc):
        sl = slice(c * fc, (c + 1) * fc)
        hh = _silu(_dot(xb, wg_ref[:, sl])) * _dot(xb, wu_ref[:, sl])
        acc = acc + _dot(hh.astype(BF16), wd_ref[sl, :])
    o_ref[...] = _layernorm(ALPHA * x + acc, g2_ref[...], b2_ref[...])


def _ev_out_ffn(x2, u, att, w1, w2, g1, b1, wg, wu, wd, g2, b2, tm=512, fc=256):
    T = x2.shape[0]
    row = lambda i: (i, 0)
    consts = (w1, w2, g1, b1, wg, wu, wd, g2, b2)
    return pl.pallas_call(
        functools.partial(_ev_out_ffn_kernel, fc=fc),
        grid=(T // tm,),
        in_specs=[
            pl.BlockSpec((tm, D_MODEL), row),
            pl.BlockSpec((tm, CONV_CH), row),
            pl.BlockSpec((tm, MLA_HEADS * MLA_V), row),
        ] + [_const_spec(c.shape) for c in consts],
        out_specs=pl.BlockSpec((tm, D_MODEL), row),
        out_shape=jax.ShapeDtypeStruct((T, D_MODEL), F32),
        compiler_params=_cparams(("parallel",)),
        name="ev_out_ffn",
    )(x2, u, att, *consts)


def _od_in_kernel(x_ref, w_ref, wkt_ref, o_ref, kt_ref):
    xb = x_ref[...].astype(BF16)
    o_ref[...] = _dot(xb, w_ref[...])
    kt_ref[...] = _dot_nt(wkt_ref[...], xb) * (ML_HEAD_DIM ** -0.5)


def _od_in(x2, w, wkt, seq, tm=512):
    T = x2.shape[0]
    nps = seq // tm
    row = lambda i: (i, 0)
    return pl.pallas_call(
        _od_in_kernel,
        grid=(T // tm,),
        in_specs=[pl.BlockSpec((tm, D_MODEL), row), _const_spec(w.shape), _const_spec(wkt.shape)],
        out_specs=[pl.BlockSpec((tm, OD_COLS), row),
                   pl.BlockSpec((ML_INNER, tm), lambda i: (i // nps, i % nps))],
        out_shape=[jax.ShapeDtypeStruct((T, OD_COLS), F32),
                   jax.ShapeDtypeStruct((T // seq * ML_INNER, seq), F32)],
        compiler_params=_cparams(("parallel",)),
        name="od_in",
    )(x2, w, wkt)


def _tri(reverse):
    i = lax.broadcasted_iota(jnp.int32, (CHUNK, CHUNK), 0)
    j = lax.broadcasted_iota(jnp.int32, (CHUNK, CHUNK), 1)
    mask = (j >= i) if reverse else (j <= i)
    return mask, mask.astype(F32)


def _softplus(v):
    return jnp.maximum(v, 0.0) + jnp.log1p(jnp.exp(-jnp.abs(v)))


def _ssd_body(xbc_ref, sm_ref, bias_ref, a_ref, y_ref, st_ref, *, reverse, lane0):
    mask, tri = _tri(reverse)
    last = 0 if reverse else CHUNK - 1
    dt_all = _softplus(sm_ref[...] + bias_ref[...])
    da_all = dt_all * a_ref[...]
    cs_all = jnp.dot(tri, da_all, preferred_element_type=F32, precision=lax.Precision.HIGHEST)
    cs_t = cs_all.T
    tot_all = cs_all[last:last + 1, :]
    grow_all = jnp.exp(cs_all)
    rest_all = jnp.exp(tot_all - cs_all)
    etot_all = jnp.exp(tot_all)
    P = SSD_HEAD_DIM
    lo = lax.broadcasted_iota(jnp.int32, (CHUNK, 2 * P), 1) < P
    lo_row = lo[0:1, :]
    pairs_per_group = SSD_HEADS // SSD_GROUPS // 2
    prev = [st_ref[p] for p in range(SSD_HEADS // 2)]

    def pick(arr, l0):
        return jnp.where(lo if arr.shape[0] > 1 else lo_row, arr[:, l0:l0 + 1], arr[:, l0 + 1:l0 + 2])

    for g in range(SSD_GROUPS):
        b0 = SSD_INNER + g * SSD_STATE
        c0 = SSD_INNER + SSD_GROUPS * SSD_STATE + g * SSD_STATE
        cm = xbc_ref[:, c0:c0 + SSD_STATE].astype(BF16)
        bm_t = xbc_ref[:, b0:b0 + SSD_STATE].T.astype(BF16)
        cb = _dot(cm, bm_t)
        for pp in range(pairs_per_group):
            p = g * pairs_per_group + pp
            l0 = lane0 + 2 * p
            ps = slice(p * 2 * P, (p + 1) * 2 * P)
            xdt = xbc_ref[:, ps] * pick(dt_all, l0)
            x_lo = jnp.where(lo, xdt, 0.0).astype(BF16)
            x_hi = jnp.where(lo, 0.0, xdt).astype(BF16)
            dec0 = jnp.exp(jnp.where(mask, cs_all[:, l0:l0 + 1] - cs_t[l0:l0 + 1, :], -jnp.inf))
            dec1 = jnp.exp(jnp.where(mask, cs_all[:, l0 + 1:l0 + 2] - cs_t[l0 + 1:l0 + 2, :], -jnp.inf))
            y_diag = _dot((cb * dec0).astype(BF16), x_lo) + _dot((cb * dec1).astype(BF16), x_hi)
            y_off = _dot(cm, prev[p].astype(BF16)) * pick(grow_all, l0)
            y_ref[:, ps] = y_diag + y_off
            xw = (xdt * pick(rest_all, l0)).astype(BF16)
            st_ref[p] = pick(etot_all, l0) * prev[p] + _dot(bm_t, xw)


def _running_max(x, reverse):
    n = x.shape[0]
    row = lax.broadcasted_iota(jnp.int32, x.shape, 0)
    s = 1
    while s < n:
        if reverse:
            sh = jnp.where(row < n - s, pltpu.roll(x, n - s, 0), -jnp.inf)
        else:
            sh = jnp.where(row >= s, pltpu.roll(x, s, 0), -jnp.inf)
        x = jnp.maximum(x, sh)
        s *= 2
    return x


def _mlstm_body(q_ref, kt_ref, v_ref, ig_ref, fg_ref, ib_ref, fb_ref, h_ref, cn_ref, m_ref, *, reverse, direction):
    L = CHUNK
    mask, tri = _tri(reverse)
    last = 0 if reverse else L - 1
    li = ig_ref[...] + ib_ref[...]
    pre = fg_ref[...] + fb_ref[...]
    lf = jnp.minimum(pre, 0.0) - jnp.log1p(jnp.exp(-jnp.abs(pre)))
    bc = jnp.dot(tri, lf, preferred_element_type=F32, precision=lax.Precision.HIGHEST)
    u = li - bc
    cm = _running_max(u, reverse)
    m_prev = m_ref[...]
    big_m = jnp.maximum(m_prev, cm)
    g = bc[last:last + 1, :]
    m_loc = g + cm[last:last + 1, :]
    u_t = u.T
    e_end_t = jnp.exp(g + u - m_loc).T
    m_new = jnp.maximum(g + m_prev, m_loc)
    a_old = jnp.exp(g + m_prev - m_new)
    a_new = jnp.exp(m_loc - m_new)
    w_inter = jnp.exp(m_prev - big_m)
    emt = jnp.exp(-(bc + big_m))
    m_ref[...] = m_new

    lane = lax.broadcasted_iota(jnp.int32, (L, 128), 1)
    lo = lane < ML_HEAD_DIM
    row = lax.broadcasted_iota(jnp.int32, (128, 2 * 128), 0)
    col = lax.broadcasted_iota(jnp.int32, (128, 2 * 128), 1)
    block_diag = (row < ML_HEAD_DIM) == ((col % 128) < ML_HEAD_DIM)
    row_lo = lax.broadcasted_iota(jnp.int32, (128, L), 0) < ML_HEAD_DIM
    ones = jnp.ones((L, 128), F32)
    npairs = ML_HEADS // 2
    prev = [cn_ref[p] for p in range(npairs)]
    for p in range(npairs):
        l0 = direction * ML_HEADS + 2 * p
        l1 = l0 + 1
        ps = slice(p * 128, (p + 1) * 128)
        qp = q_ref[:, ps]
        vp = v_ref[:, ps]
        kt = kt_ref[ps, :]
        ktb = kt.astype(BF16)
        q_lo = jnp.where(lo, qp, 0.0).astype(BF16)
        q_hi = jnp.where(lo, 0.0, qp).astype(BF16)
        vo_lo = jnp.concatenate([jnp.where(lo, vp, 0.0), jnp.where(lo, ones, 0.0)], -1).astype(BF16)
        vo_hi = jnp.concatenate([jnp.where(lo, 0.0, vp), jnp.where(lo, 0.0, ones)], -1).astype(BF16)
        w0 = jnp.where(mask, jnp.exp(u_t[l0:l0 + 1, :] - big_m[:, l0:l0 + 1]), 0.0)
        w1 = jnp.where(mask, jnp.exp(u_t[l1:l1 + 1, :] - big_m[:, l1:l1 + 1]), 0.0)
        a0 = (_dot(q_lo, ktb) * w0).astype(BF16)
        a1 = (_dot(q_hi, ktb) * w1).astype(BF16)
        wi = jnp.where(lo, w_inter[:, l0:l0 + 1], w_inter[:, l1:l1 + 1])
        wi2 = jnp.concatenate([wi, wi], -1)
        nd = _dot(a0, vo_lo) + _dot(a1, vo_hi) + wi2 * _dot(qp.astype(BF16), prev[p].astype(BF16))
        floor = jnp.where(lo, emt[:, l0:l0 + 1], emt[:, l1:l1 + 1])
        h_ref[:, ps] = nd[:, :128] / jnp.maximum(jnp.abs(nd[:, 128:]), floor)
        e_t = jnp.where(row_lo, e_end_t[l0:l0 + 1, :], e_end_t[l1:l1 + 1, :])
        kte = (kt * e_t).astype(BF16)
        vo = jnp.concatenate([vp, ones], -1).astype(BF16)
        s_loc = jnp.where(block_diag, _dot(kte, vo), 0.0)
        row2 = lax.broadcasted_iota(jnp.int32, (128, 1), 0) < ML_HEAD_DIM
        ao = jnp.where(row2, a_old[:, l0:l0 + 1], a_old[:, l1:l1 + 1])
        an = jnp.where(row2, a_new[:, l0:l0 + 1], a_new[:, l1:l1 + 1])
        cn_ref[p] = ao * prev[p] + an * s_loc


def _odd_mix_kernel(xbc_f, dt_f, q_f, kt_f, v_f, ig_f, fg_f, xbc_r, dt_r, q_r, kt_r, v_r, ig_r, fg_r,
                    dtb_f, a_f, dtb_r, a_r, ib_ref, fb_ref, yf_ref, yb_ref, hf_ref, hb_ref,
                    st_f, st_r, cn_f, cn_r, m_f, m_r):
    @pl.when(pl.program_id(1) == 0)
    def _():
        for ref in (st_f, st_r, cn_f, cn_r, m_f, m_r):
            ref[...] = jnp.zeros(ref.shape, F32)

    _ssd_body(xbc_f, dt_f, dtb_f, a_f, yf_ref, st_f, reverse=False, lane0=0)
    _ssd_body(xbc_r, dt_r, dtb_r, a_r, yb_ref, st_r, reverse=True, lane0=SSD_HEADS)
    _mlstm_body(q_f, kt_f, v_f, ig_f, fg_f, ib_ref, fb_ref, hf_ref, cn_f, m_f, reverse=False, direction=0)
    _mlstm_body(q_r, kt_r, v_r, ig_r, fg_r, ib_ref, fb_ref, hb_ref, cn_r, m_r, reverse=True, direction=1)


def _odd_mix(xbc, proj, kt, S, w):
    T = proj.shape[0]
    B = T // S
    nc = S // CHUNK
    fwd = lambda c: c
    rev = lambda c: nc - 1 - c

    def specs(cidx):
        col = lambda cb: (lambda b, c: (b * nc + cidx(c), cb))
        return [
            pl.BlockSpec((CHUNK, SSD_XBC), col(0)),
            pl.BlockSpec((CHUNK, 128), col(OD_DT // 128)),
            pl.BlockSpec((CHUNK, ML_INNER), col(OD_Q // 512)),
            pl.BlockSpec((ML_INNER, CHUNK), lambda b, c: (b, cidx(c))),
            pl.BlockSpec((CHUNK, ML_INNER), col(OD_V // 512)),
            pl.BlockSpec((CHUNK, 128), col(OD_IG // 128)),
            pl.BlockSpec((CHUNK, 128), col(OD_FG // 128)),
        ]

    rows = [w["dt_bias"][0], w["a"][0], w["dt_bias"][1], w["a"][1], w["ig_b"], w["fg_b"]]
    out = lambda cidx: pl.BlockSpec((CHUNK, 512), lambda b, c: (b * nc + cidx(c), 0))
    seq = (xbc, proj, proj, kt, proj, proj, proj)
    return pl.pallas_call(
        _odd_mix_kernel,
        grid=(B, nc),
        in_specs=specs(fwd) + specs(rev) + [_const_spec(r.shape) for r in rows],
        out_specs=[out(fwd), out(rev), out(fwd), out(rev)],
        out_shape=[jax.ShapeDtypeStruct((T, 512), F32)] * 4,
        scratch_shapes=[
            pltpu.VMEM((SSD_HEADS // 2, SSD_STATE, 2 * SSD_HEAD_DIM), F32),
            pltpu.VMEM((SSD_HEADS // 2, SSD_STATE, 2 * SSD_HEAD_DIM), F32),
            pltpu.VMEM((ML_HEADS // 2, 128, 256), F32),
            pltpu.VMEM((ML_HEADS // 2, 128, 256), F32),
            pltpu.VMEM((1, 128), F32),
            pltpu.VMEM((1, 128), F32),
        ],
        compiler_params=_cparams(("parallel", "arbitrary")),
        name="odd_mix",
    )(*seq, *seq, *rows)


def _od_out_kernel(x_ref, z_ref, xs_ref, o_ref, yf_ref, yb_ref, hf_ref, hb_ref, dsk_ref, sg_ref, mg_ref, avg_ref,
                   w1_ref, w2_ref, g_ref, b_ref, rw_ref, rb_ref, x1_ref, route_ref, route_t_ref):
    y = (yf_ref[...] + yb_ref[...] + xs_ref[...] * dsk_ref[...]) * _silu(z_ref[...])
    gw = SSD_INNER // SSD_GROUPS
    m = None
    for g in range(SSD_GROUPS):
        sl = slice(g * gw, (g + 1) * gw)
        yn = _rmsnorm(y[:, sl], sg_ref[:, sl]).astype(BF16)
        t = _dot(yn, w1_ref[sl, :])
        m = t if m is None else m + t
    hs = hf_ref[...] + hb_ref[...]
    avg = avg_ref[...]

    def head_mean(v):
        hi = v.astype(BF16)
        lo = (v - hi.astype(F32)).astype(BF16)
        return _dot(hi, avg) + _dot(lo, avg)

    dv = hs - head_mean(hs)
    var = head_mean(dv * dv)
    hn = jax.nn.sigmoid(o_ref[...]) * (dv * lax.rsqrt(var + LN_EPS) * mg_ref[...])
    m = m + _dot(hn.astype(BF16), w2_ref[...])
    x1 = _layernorm(ALPHA * x_ref[...] + m, g_ref[...], b_ref[...])
    x1_ref[...] = x1
    logits = _dot(x1.astype(BF16), rw_ref[...]) + rb_ref[...]
    lane = lax.broadcasted_iota(jnp.int32, logits.shape, 1).astype(F32)
    m1 = jnp.max(logits, -1, keepdims=True)
    i1 = jnp.min(jnp.where(logits == m1, lane, 128.0), -1, keepdims=True)
    rest = jnp.where(lane == i1, -jnp.inf, logits)
    m2 = jnp.max(rest, -1, keepdims=True)
    i2 = jnp.min(jnp.where(rest == m2, lane, 128.0), -1, keepdims=True)
    e = jnp.exp(m2 - m1)
    g1 = 1.0 / (1.0 + e)
    g2 = e / (1.0 + e)
    route = jnp.where(lane == 0.0, i1,
                      jnp.where(lane == 1.0, i2, jnp.where(lane == 2.0, g1, jnp.where(lane == 3.0, g2, 0.0))))
    route_ref[...] = route
    route_t_ref[...] = route.T[0:8, :]


def _od_out(x2, proj, xbc, yf, yb, hf, hb, dsk, sg, mg, w1, w2, g, b, rw, rb, tm=256):
    T = x2.shape[0]
    row = lambda i: (i, 0)
    colb = lambda cb: (lambda i: (i, cb))
    head = jnp.arange(ML_INNER) // ML_HEAD_DIM
    avg = jnp.where(head[:, None] == head[None, :], 1.0 / ML_HEAD_DIM, 0.0).astype(BF16)
    return pl.pallas_call(
        _od_out_kernel,
        grid=(T // tm,),
        in_specs=[
            pl.BlockSpec((tm, D_MODEL), row),
            pl.BlockSpec((tm, 512), colb(OD_Z // 512)),
            pl.BlockSpec((tm, 512), colb(0)),
            pl.BlockSpec((tm, 512), colb(OD_O // 512)),
            pl.BlockSpec((tm, 512), row), pl.BlockSpec((tm, 512), row),
            pl.BlockSpec((tm, 512), row), pl.BlockSpec((tm, 512), row),
            _const_spec(dsk.shape), _const_spec(sg.shape), _const_spec(mg.shape), _const_spec(avg.shape),
            _const_spec(w1.shape), _const_spec(w2.shape), _const_spec(g.shape), _const_spec(b.shape),
            _const_spec(rw.shape), _const_spec(rb.shape),
        ],
        out_specs=[pl.BlockSpec((tm, D_MODEL), row), pl.BlockSpec((tm, 128), row),
                   pl.BlockSpec((8, tm), lambda i: (0, i))],
        out_shape=[jax.ShapeDtypeStruct((T, D_MODEL), F32), jax.ShapeDtypeStruct((T, 128), F32),
                   jax.ShapeDtypeStruct((8, T), F32)],
        compiler_params=_cparams(("parallel",)),
        name="od_out",
    )(x2, proj, xbc, proj, yf, yb, hf, hb, dsk, sg, mg, avg, w1, w2, g, b, rw, rb)


def _row_copy(src_hbm, idx, dst, r, sem):
    return pltpu.make_async_copy(src_hbm.at[pl.ds(idx, 1)], dst.at[pl.ds(r, 1)], sem)


def _gather_rows(src_hbm, idx_smem, dst, sem, n):
    for r in range(n):
        _row_copy(src_hbm, idx_smem[0, r], dst, r, sem).start(priority=r % 2)


def _gather_wait(src_hbm, dst, sem):
    pltpu.make_async_copy(src_hbm.at[pl.ds(0, dst.shape[0])], dst, sem).wait()


def _gather_loop(src_hbm, idx_smem, idx0, dst, sem):
    def start(r, carry):
        _row_copy(src_hbm, idx_smem[0, idx0 + r], dst, r, sem).start()
        return carry

    lax.fori_loop(0, dst.shape[0], start, 0, unroll=8)


def _moe_kernel(te_ref, tok0_ref, tokn_ref, x_hbm, wg_ref, wu_ref, wd_ref, o_ref,
                xbuf, xb_ref, acc_ref, sem, *, nfc, sub):
    i = pl.program_id(0)
    j = pl.program_id(1)
    n = pl.num_programs(0)
    per_step = xbuf.shape[1]

    @pl.when(jnp.logical_and(i == 0, j == 0))
    def _():
        for jj in range(nfc):
            _gather_loop(x_hbm, tok0_ref, jj * per_step, xbuf.at[jj], sem)

    @pl.when(j == 0)
    def _():
        for jj in range(nfc):
            _gather_wait(x_hbm, xbuf.at[jj], sem)
        for jj in range(nfc):
            xb_ref[jj * per_step:(jj + 1) * per_step, :] = xbuf[jj].astype(BF16)
        acc_ref[...] = jnp.zeros(acc_ref.shape, F32)

    xb = xb_ref[...]
    acc = acc_ref[...]
    for c in range(wg_ref.shape[1] // sub):
        sl = slice(c * sub, (c + 1) * sub)
        hh = _silu(_dot(xb, wg_ref[:, sl])) * _dot(xb, wu_ref[:, sl])
        acc = acc + _dot(hh.astype(BF16), wd_ref[sl, :])
    acc_ref[...] = acc
    _gather_rows(x_hbm, tokn_ref, xbuf.at[j], sem, per_step)

    @pl.when(j == nfc - 1)
    def _():
        o_ref[...] = acc_ref[...]

    @pl.when(jnp.logical_and(i == n - 1, j == nfc - 1))
    def _():
        for jj in range(nfc):
            _gather_wait(x_hbm, xbuf.at[jj], sem)


def _moe(x1, tile_expert, row_token, wg, wu, wd, tm, fc=D_FF_EXPERT, sub=256):
    n_tiles = tile_expert.shape[0]
    assert D_FF_EXPERT % fc == 0 and fc % sub == 0
    nfc = D_FF_EXPERT // fc
    per_step = tm // nfc
    wmode = pl.Buffered(1) if nfc == 1 else pl.Buffered(2)
    grid_spec = pltpu.PrefetchScalarGridSpec(
        num_scalar_prefetch=1,
        grid=(n_tiles, nfc),
        in_specs=[
            pl.BlockSpec((None, 1, tm), lambda i, j, te: (0, 0, 0), memory_space=pltpu.SMEM),
            pl.BlockSpec((None, 1, per_step), lambda i, j, te: ((i + 1) * nfc + j, 0, 0), memory_space=pltpu.SMEM),
            pl.BlockSpec(memory_space=pl.ANY),
            pl.BlockSpec((None, D_MODEL, fc), lambda i, j, te: (te[i], 0, j), pipeline_mode=wmode),
            pl.BlockSpec((None, D_MODEL, fc), lambda i, j, te: (te[i], 0, j), pipeline_mode=wmode),
            pl.BlockSpec((None, fc, D_MODEL), lambda i, j, te: (te[i], j, 0), pipeline_mode=wmode),
        ],
        out_specs=pl.BlockSpec((tm, D_MODEL), lambda i, j, te: (i, 0)),
        scratch_shapes=[
            pltpu.VMEM((nfc, per_step, D_MODEL), F32),
            pltpu.VMEM((tm, D_MODEL), BF16),
            pltpu.VMEM((tm, D_MODEL), F32),
            pltpu.SemaphoreType.DMA,
        ],
    )
    return pl.pallas_call(
        functools.partial(_moe_kernel, nfc=nfc, sub=sub),
        grid_spec=grid_spec,
        out_shape=jax.ShapeDtypeStruct((n_tiles * tm, D_MODEL), F32),
        compiler_params=_cparams(("arbitrary", "arbitrary")),
        name="moe_experts",
    )(tile_expert, row_token.reshape(n_tiles + 1, 1, tm), row_token.reshape((n_tiles + 1) * nfc, 1, per_step),
      x1, wg, wu, wd)


def _combine_kernel(pos0_ref, posa_ref, posb_ref, x_ref, route_ref, y_hbm, g_ref, b_ref, o_ref, ybuf, sem, *, tm):
    s = pl.program_id(0)
    n = pl.num_programs(0)

    @pl.when(s == 0)
    def _():
        for half in range(2):
            _gather_loop(y_hbm, pos0_ref, half * 2 * tm, ybuf.at[half], sem.at[half])

    for half, pos_ref in ((0, posa_ref), (1, posb_ref)):
        _gather_wait(y_hbm, ybuf.at[half], sem.at[half])
        rows = slice(half * tm, (half + 1) * tm)
        g0 = route_ref[rows, 2:3]
        g1 = route_ref[rows, 3:4]
        f = g0 * ybuf[half, 0:tm, :] + g1 * ybuf[half, tm:2 * tm, :]
        _gather_rows(y_hbm, pos_ref, ybuf.at[half], sem.at[half], 2 * tm)
        o_ref[rows, :] = _layernorm(ALPHA * x_ref[rows, :] + f, g_ref[...], b_ref[...])

    @pl.when(s == n - 1)
    def _():
        _gather_wait(y_hbm, ybuf.at[0], sem.at[0])
        _gather_wait(y_hbm, ybuf.at[1], sem.at[1])


def _combine(x1, route, y_sorted, pos, g, b, tm=256):
    T = route.shape[0]
    nt = T // tm
    assert nt % 2 == 0
    return pl.pallas_call(
        functools.partial(_combine_kernel, tm=tm),
        grid=(nt // 2,),
        in_specs=[
            pl.BlockSpec((None, 1, 4 * tm), lambda s: (0, 0, 0), memory_space=pltpu.SMEM),
            pl.BlockSpec((None, 1, 2 * tm), lambda s: (2 * s + 2, 0, 0), memory_space=pltpu.SMEM),
            pl.BlockSpec((None, 1, 2 * tm), lambda s: (2 * s + 3, 0, 0), memory_space=pltpu.SMEM),
            pl.BlockSpec((2 * tm, D_MODEL), lambda s: (s, 0)),
            pl.BlockSpec((2 * tm, 128), lambda s: (s, 0)),
            pl.BlockSpec(memory_space=pl.ANY),
            _const_spec(g.shape), _const_spec(b.shape),
        ],
        out_specs=pl.BlockSpec((2 * tm, D_MODEL), lambda s: (s, 0)),
        out_shape=jax.ShapeDtypeStruct((T, D_MODEL), F32),
        scratch_shapes=[pltpu.VMEM((2, 2 * tm, D_MODEL), F32), pltpu.SemaphoreType.DMA((2,))],
        compiler_params=_cparams(("arbitrary",)),
        name="moe_combine",
    )(pos.reshape((nt + 2) // 2, 1, 4 * tm), pos, pos, x1, route, y_sorted, g, b)


def _route_tables(route_t, tm_e, tm_c):
    T = route_t.shape[1]
    A = 2 * T
    e0 = route_t[0].astype(jnp.int32)
    e1 = route_t[1].astype(jnp.int32)
    ids = jnp.arange(N_EXPERTS, dtype=jnp.int32)[:, None]
    oh0 = e0[None, :] == ids
    oh1 = e1[None, :] == ids
    c0 = jnp.cumsum(oh0.astype(jnp.int32), axis=1)
    c1 = jnp.cumsum(oh1.astype(jnp.int32), axis=1)
    n0 = c0[:, -1:]
    counts = (n0 + c1[:, -1:])[:, 0]
    start = jnp.cumsum(counts) - counts
    padded = ((counts + tm_e - 1) // tm_e) * tm_e
    pend = jnp.cumsum(padded)
    pstart = pend - padded
    dest0 = jnp.sum(jnp.where(oh0, pstart[:, None] + c0 - 1, 0), axis=0)
    dest1 = jnp.sum(jnp.where(oh1, pstart[:, None] + n0 + c1 - 1, 0), axis=0)
    order = jnp.argsort(jnp.concatenate([e0, e1]), stable=True).astype(jnp.int32)
    n_tiles = A // tm_e + N_EXPERTS
    rows = jnp.arange(n_tiles * tm_e, dtype=jnp.int32)
    past = rows[None, :] >= pend[:, None]
    e_row = jnp.minimum(jnp.sum(past.astype(jnp.int32), axis=0), N_EXPERTS - 1)
    local = rows - jnp.sum(jnp.where(past, padded[:, None], 0), axis=0)
    count_row = jnp.sum(jnp.where(e_row[None, :] == ids, counts[:, None], 0), axis=0)
    ok = jnp.logical_and(local < count_row, rows < pend[-1])
    src = jnp.sum(jnp.where(past, counts[:, None], 0), axis=0) + local
    a_row = order[jnp.clip(src, 0, A - 1)]
    row_token = jnp.where(ok, jnp.where(a_row >= T, a_row - T, a_row), 0).astype(jnp.int32)
    row_token = jnp.concatenate([row_token, jnp.zeros((tm_e,), jnp.int32)])
    tile_expert = e_row[::tm_e]
    nt = T // tm_c
    pos = jnp.concatenate([dest0.reshape(nt, tm_c), dest1.reshape(nt, tm_c)], axis=1)
    pos = jnp.concatenate([pos, jnp.zeros((2, 2 * tm_c), jnp.int32)], 0).astype(jnp.int32)
    return tile_expert, row_token, pos.reshape(nt + 2, 1, 2 * tm_c)


def _rot_cols(w):
    half = MLA_ROPE // 2
    return jnp.concatenate([-w[..., half:], w[..., :half]], -1)


def _prep_even(p, j):
    w_in = p["ev_w_in"][j]
    c_rot = 2 * CONV_CH + MLA_Q_LORA + MLA_KV_LORA
    k_rot = w_in[:, c_rot:c_rot + MLA_ROPE]
    z64 = jnp.zeros((D_MODEL, MLA_NOPE), F32)
    z32 = jnp.zeros((D_MODEL, HEAD_PAD - MLA_NOPE - MLA_ROPE), F32)
    w_in2 = jnp.concatenate([w_in[:, :c_rot], z64, k_rot, z32, z64, _rot_cols(k_rot), z32], -1).astype(BF16)
    wq = p["mla_w_uq"][j].reshape(MLA_Q_LORA, MLA_HEADS, MLA_NOPE + MLA_ROPE)
    zq = jnp.zeros((MLA_Q_LORA, MLA_HEADS, HEAD_PAD - MLA_NOPE - MLA_ROPE), F32)
    zq64 = jnp.zeros((MLA_Q_LORA, MLA_HEADS, MLA_NOPE), F32)
    wq_plain = jnp.concatenate([wq, zq], -1).reshape(MLA_Q_LORA, -1)
    wq_rot = jnp.concatenate([zq64, _rot_cols(wq[..., MLA_NOPE:]), zq], -1).reshape(MLA_Q_LORA, -1)
    wq2 = jnp.concatenate([wq_plain, wq_rot], -1).astype(BF16)
    wkv = p["mla_w_ukv"][j].reshape(MLA_KV_LORA, MLA_HEADS, MLA_NOPE + MLA_V)
    zk = jnp.zeros((MLA_KV_LORA, MLA_HEADS, MLA_V), F32)
    wk = jnp.concatenate([wkv[..., :MLA_NOPE], zk], -1).reshape(MLA_KV_LORA, -1)
    wv = wkv[..., MLA_NOPE:]
    even = (jnp.arange(MLA_HEADS) % 2 == 0)[None, :, None]
    wv2 = jnp.concatenate([jnp.where(even, wv, 0.0), jnp.where(even, 0.0, wv)], -1).reshape(MLA_KV_LORA, -1)
    wkv2 = jnp.concatenate([wk, wv2], -1).astype(BF16)
    w_out = p["ev_w_out"][j].astype(BF16)
    return dict(
        w_in=w_in2, qg=p["mla_q_norm_g"][j][None], wq=wq2, kvg=p["mla_kv_norm_g"][j][None], wkv=wkv2,
        dw_w=p["conv_dw_w"][j], dw_b=p["conv_dw_b"][j][None], cln_g=p["conv_ln_g"][j][None],
        cln_b=p["conv_ln_b"][j][None], w1=w_out[:CONV_CH], w2=w_out[CONV_CH:],
        wg=p["ffn_w_gate"][j].astype(BF16), wu=p["ffn_w_up"][j].astype(BF16), wd=p["ffn_w_down"][j].astype(BF16),
    )


def _lane_row(vals, lane0):
    return jnp.zeros((1, 128), F32).at[0, lane0:lane0 + vals.shape[0]].set(vals)


def _prep_odd(p, j):
    w = p["od_w_in"][j]
    z112 = jnp.zeros((D_MODEL, 128 - 16), F32)
    w_in2 = jnp.concatenate([w[:, 0:1536], w[:, 1552:2064], w[:, 2576:3600],
                             w[:, 1536:1552], z112, w[:, 3600:3616], z112, w[:, 3616:3632], z112], -1).astype(BF16)
    wkt = w[:, 2064:2576].T.astype(BF16)
    w_out = p["od_w_out"][j].astype(BF16)
    rw = jnp.concatenate([p["moe_router_w"][j], jnp.zeros((D_MODEL, 128 - N_EXPERTS), F32)], -1).astype(BF16)
    rb = jnp.full((1, 128), -jnp.inf, F32).at[0, :N_EXPERTS].set(p["moe_router_b"][j])
    a = -jnp.exp(p["ssd_a_log"][j])
    return dict(
        w_in=w_in2, wkt=wkt, cw=p["ssd_conv_w"][j], cb=p["ssd_conv_b"][j][None],
        dt_bias=[_lane_row(p["ssd_dt_bias"][j][d], d * SSD_HEADS) for d in range(2)],
        a=[_lane_row(a[d], d * SSD_HEADS) for d in range(2)],
        ig_b=_lane_row(p["ml_igate_b"][j].reshape(-1), 0), fg_b=_lane_row(p["ml_fgate_b"][j].reshape(-1), 0),
        dsk=jnp.repeat(p["ssd_d"][j], SSD_HEAD_DIM)[None], sg=p["ssd_norm_g"][j][None], mg=p["ml_norm_g"][j][None],
        w1=w_out[:SSD_INNER], w2=w_out[SSD_INNER:], rw=rw, rb=rb,
        wg=p["moe_w_gate"][j].astype(BF16), wu=p["moe_w_up"][j].astype(BF16), wd=p["moe_w_down"][j].astype(BF16),
    )


def _rope_tables(seq):
    half = MLA_ROPE // 2
    inv_freq = ROPE_THETA ** (-jnp.arange(half, dtype=F32) / half)
    ang = jnp.arange(seq, dtype=F32)[:, None] * inv_freq
    cos2 = jnp.concatenate([jnp.cos(ang), jnp.cos(ang)], -1)
    sin2 = jnp.concatenate([jnp.sin(ang), jnp.sin(ang)], -1)
    pad = jnp.zeros((seq, HEAD_PAD - MLA_NOPE - MLA_ROPE), F32)
    cos_t = jnp.concatenate([jnp.ones((seq, MLA_NOPE), F32), cos2, pad], -1)
    sin_t = jnp.concatenate([jnp.zeros((seq, MLA_NOPE), F32), sin2, pad], -1)
    return cos_t, sin_t


def _even_layer(x2, B, S, w, ln, cos_t, sin_t):
    tm = min(512, S)
    u, q, k, v = _ev_in(x2, cos_t, sin_t, w["w_in"], w["qg"], w["wq"], w["kvg"], w["wkv"], S, tm=tm)
    uc = _dwconv(u, S, 0, w["dw_w"], w["dw_b"], w["cln_g"], w["cln_b"], with_ln=True, out_dtype=BF16)
    att = _attention(q, k, v, B, S, tq=min(256, S))
    return _ev_out_ffn(x2, uc, att, w["w1"], w["w2"], ln[0], ln[1], w["wg"], w["wu"], w["wd"], ln[2], ln[3], tm=tm)


def _odd_layer(x2, B, S, w, ln, tm_e=512, tm_c=256):
    proj, kt = _od_in(x2, w["w_in"], w["wkt"], S, tm=min(512, S))
    zeros = jnp.zeros((1, SSD_XBC), F32)
    xbc = _dwconv(proj, S, OD_XBC // 512, w["cw"], w["cb"], zeros, zeros, with_ln=False, out_dtype=F32, ncb=2)
    yf, yb, hf, hb = _odd_mix(xbc, proj, kt, S, w)
    x1, route, route_t = _od_out(x2, proj, xbc, yf, yb, hf, hb, w["dsk"], w["sg"], w["mg"],
                                 w["w1"], w["w2"], ln[0], ln[1], w["rw"], w["rb"])
    te, row_token, pos = _route_tables(route_t, tm_e, tm_c)
    y_sorted = _moe(x1, te, row_token, w["wg"], w["wu"], w["wd"], tm_e)
    return _combine(x1, route, y_sorted, pos, ln[2], ln[3], tm_c)


def _trunk(x, p):
    B, S, _ = x.shape
    x2 = x.reshape(B * S, D_MODEL)
    cos_t, sin_t = _rope_tables(S)
    for l in range(DEPTH):
        j = l // 2
        ln = (p["ln1_g"][l][None], p["ln1_b"][l][None], p["ln2_g"][l][None], p["ln2_b"][l][None])
        if l % 2 == 0:
            x2 = _even_layer(x2, B, S, _prep_even(p, j), ln, cos_t, sin_t)
        else:
            x2 = _odd_layer(x2, B, S, _prep_odd(p, j), ln)
    return x2.reshape(B, S, D_MODEL)


def kernel(x_prompt, x_sample, ev_w_in, conv_dw_w, conv_dw_b, conv_ln_g, conv_ln_b, mla_q_norm_g, mla_w_uq, mla_kv_norm_g, mla_w_ukv, ev_w_out, od_w_in, ssd_conv_w, ssd_conv_b, ssd_dt_bias, ssd_a_log, ssd_d, ssd_norm_g, ml_igate_b, ml_fgate_b, ml_norm_g, od_w_out, ffn_w_gate, ffn_w_up, ffn_w_down, moe_router_w, moe_router_b, moe_w_gate, moe_w_up, moe_w_down, ln1_g, ln1_b, ln2_g, ln2_b):
    p = dict(ev_w_in=ev_w_in, conv_dw_w=conv_dw_w, conv_dw_b=conv_dw_b, conv_ln_g=conv_ln_g, conv_ln_b=conv_ln_b,
             mla_q_norm_g=mla_q_norm_g, mla_w_uq=mla_w_uq, mla_kv_norm_g=mla_kv_norm_g, mla_w_ukv=mla_w_ukv,
             ev_w_out=ev_w_out, od_w_in=od_w_in, ssd_conv_w=ssd_conv_w, ssd_conv_b=ssd_conv_b,
             ssd_dt_bias=ssd_dt_bias, ssd_a_log=ssd_a_log, ssd_d=ssd_d, ssd_norm_g=ssd_norm_g,
             ml_igate_b=ml_igate_b, ml_fgate_b=ml_fgate_b, ml_norm_g=ml_norm_g, od_w_out=od_w_out,
             ffn_w_gate=ffn_w_gate, ffn_w_up=ffn_w_up, ffn_w_down=ffn_w_down, moe_router_w=moe_router_w,
             moe_router_b=moe_router_b, moe_w_gate=moe_w_gate, moe_w_up=moe_w_up, moe_w_down=moe_w_down,
             ln1_g=ln1_g, ln1_b=ln1_b, ln2_g=ln2_g, ln2_b=ln2_b)
    assert x_prompt.shape[1] == x_sample.shape[1]
    nb = x_prompt.shape[0]
    y = _trunk(jnp.concatenate([x_prompt, x_sample], 0), p)
    return (y[:nb], y[nb:])
```

```python
import functools
import math

import jax
import jax.numpy as jnp
from jax import lax
from jax.experimental import pallas as pl
from jax.experimental.pallas import tpu as pltpu

F32 = jnp.float32
BF16 = jnp.bfloat16

D_MODEL = 1024
DEPTH = 4
ALPHA = (2.0 * DEPTH) ** 0.25
LN_EPS = 1e-5
RMS_EPS = 1e-6

CONV_CH = 512
MLA_HEADS = 8
MLA_NOPE = 64
MLA_ROPE = 32
MLA_V = 64
MLA_Q_LORA = 256
MLA_KV_LORA = 128
ROPE_THETA = 10000.0
HEAD_PAD = 128
Q_SCALE = (MLA_NOPE + MLA_ROPE) ** -0.5 * math.log2(math.e)

SSD_HEADS = 8
SSD_HEAD_DIM = 64
SSD_INNER = 512
SSD_GROUPS = 2
SSD_STATE = 128
SSD_XBC = 1024
CHUNK = 128
ML_HEADS = 8
ML_HEAD_DIM = 64
ML_INNER = 512
OD_Z, OD_XBC, OD_Q, OD_V, OD_O, OD_DT, OD_IG, OD_FG = 0, 512, 1536, 2048, 2560, 3072, 3200, 3328
OD_COLS = 3456

N_EXPERTS = 8
D_FF_EXPERT = 3584

VMEM_LIMIT = 56 * 1024 * 1024


def _cparams(sem):
    return pltpu.CompilerParams(dimension_semantics=sem, vmem_limit_bytes=VMEM_LIMIT)


def _const_spec(shape):
    nd = len(shape)
    return pl.BlockSpec(shape, lambda *_: (0,) * nd, pipeline_mode=pl.Buffered(1))


def _layernorm(v, g, b):
    mu = jnp.mean(v, -1, keepdims=True)
    d = v - mu
    var = jnp.mean(d * d, -1, keepdims=True)
    return d * lax.rsqrt(var + LN_EPS) * g + b


def _rmsnorm(v, g):
    return v * lax.rsqrt(jnp.mean(v * v, -1, keepdims=True) + RMS_EPS) * g


def _silu(v):
    return v * jax.nn.sigmoid(v)


def _dot(a, b):
    return jnp.dot(a, b, preferred_element_type=F32)


def _dot_nt(a, b):
    return lax.dot_general(a, b, (((1,), (1,)), ((), ())), preferred_element_type=F32)


def _ev_in_kernel(x_ref, cos_ref, sin_ref, w_in_ref, qg_ref, wq_ref, kvg_ref, wkv_ref,
                  u_ref, q_ref, k_ref, v_ref):
    xb = x_ref[...].astype(BF16)
    h = _dot(xb, w_in_ref[...])
    u_ref[...] = h[:, :CONV_CH] * jax.nn.sigmoid(h[:, CONV_CH:2 * CONV_CH])
    c0 = 2 * CONV_CH
    cos = cos_ref[...]
    sin = sin_ref[...]
    ql = _rmsnorm(h[:, c0:c0 + MLA_Q_LORA], qg_ref[...]).astype(BF16)
    qq = _dot(ql, wq_ref[...])
    c1 = c0 + MLA_Q_LORA
    kvl = _rmsnorm(h[:, c1:c1 + MLA_KV_LORA], kvg_ref[...]).astype(BF16)
    kk = _dot(kvl, wkv_ref[...])
    c2 = c1 + MLA_KV_LORA
    kpe = h[:, c2:c2 + HEAD_PAD] * cos + h[:, c2 + HEAD_PAD:c2 + 2 * HEAD_PAD] * sin
    nh = MLA_HEADS * HEAD_PAD
    for hd in range(MLA_HEADS):
        sl = slice(hd * HEAD_PAD, (hd + 1) * HEAD_PAD)
        sl2 = slice(nh + hd * HEAD_PAD, nh + (hd + 1) * HEAD_PAD)
        q_ref[:, sl] = ((qq[:, sl] * cos + qq[:, sl2] * sin) * Q_SCALE).astype(BF16)
        k_ref[:, sl] = (kk[:, sl] + kpe).astype(BF16)
    v_ref[...] = kk[:, nh:].astype(BF16)


def _ev_in(x2, cos_t, sin_t, w_in, qg, wq, kvg, wkv, seq, tm=512):
    T = x2.shape[0]
    nps = seq // tm
    row = lambda i: (i, 0)
    pos = lambda i: (i % nps, 0)
    nh = MLA_HEADS * HEAD_PAD
    return pl.pallas_call(
        _ev_in_kernel,
        grid=(T // tm,),
        in_specs=[
            pl.BlockSpec((tm, D_MODEL), row),
            pl.BlockSpec((tm, HEAD_PAD), pos),
            pl.BlockSpec((tm, HEAD_PAD), pos),
            _const_spec(w_in.shape), _const_spec(qg.shape), _const_spec(wq.shape),
            _const_spec(kvg.shape), _const_spec(wkv.shape),
        ],
        out_specs=[
            pl.BlockSpec((tm, CONV_CH), row),
            pl.BlockSpec((tm, nh), row),
            pl.BlockSpec((tm, nh), row),
            pl.BlockSpec((tm, nh), row),
        ],
        out_shape=[
            jax.ShapeDtypeStruct((T, CONV_CH), F32),
            jax.ShapeDtypeStruct((T, nh), BF16),
            jax.ShapeDtypeStruct((T, nh), BF16),
            jax.ShapeDtypeStruct((T, nh), BF16),
        ],
        compiler_params=_cparams(("parallel",)),
        name="ev_in",
    )(x2, cos_t, sin_t, w_in, qg, wq, kvg, wkv)


def _dwconv_kernel(x_ref, w_ref, b_ref, g_ref, beta_ref, o_ref, pad_ref, tmp_ref, *, width, halo, rows, with_ln):
    S, C = x_ref.shape
    half = width // 2
    win = rows + 2 * halo
    pad_ref[0:halo, :] = jnp.zeros((halo, C), F32)
    pad_ref[halo + S:halo + S + halo, :] = jnp.zeros((halo, C), F32)
    pad_ref[halo:halo + S, :] = x_ref[...]

    def tile(t, carry):
        r0 = pl.multiple_of(t * rows, rows)
        for cb in range(C // 128):
            cs = slice(cb * 128, (cb + 1) * 128)
            window = pad_ref[pl.ds(r0, win), cs]
            acc = jnp.zeros((rows, 128), F32) + b_ref[:, cs]
            for r in range(8):
                taps = [w for w in range(width) if (halo + w - half) % 8 == r]
                if taps:
                    rolled = window if r == 0 else pltpu.roll(window, win - r, 0)
                    for w in taps:
                        a0 = halo + w - half - r
                        acc = acc + rolled[a0:a0 + rows] * w_ref[w:w + 1, cs]
            tmp_ref[:, cs] = acc
        acc = tmp_ref[...]
        if with_ln:
            acc = _layernorm(acc, g_ref[...], beta_ref[...])
        o_ref[pl.ds(r0, rows), :] = _silu(acc).astype(o_ref.dtype)
        return carry

    lax.fori_loop(0, S // rows, tile, 0)


def _dwconv(x2, S, col_block, w, b, g, beta, *, with_ln, out_dtype, ncb=1, rows=128):
    B = x2.shape[0] // S
    C = 512
    width = w.shape[0]
    halo = 16
    assert width // 2 <= halo
    kern = functools.partial(_dwconv_kernel, width=width, halo=halo, rows=rows, with_ln=with_ln)
    return pl.pallas_call(
        kern,
        grid=(B, ncb),
        in_specs=[
            pl.BlockSpec((S, C), lambda b, c: (b, col_block + c)),
            pl.BlockSpec((width, C), lambda b, c: (0, c)),
            pl.BlockSpec((1, C), lambda b, c: (0, c)),
            pl.BlockSpec((1, C), lambda b, c: (0, c)),
            pl.BlockSpec((1, C), lambda b, c: (0, c)),
        ],
        out_specs=pl.BlockSpec((S, C), lambda b, c: (b, c)),
        out_shape=jax.ShapeDtypeStruct((B * S, C * ncb), out_dtype),
        scratch_shapes=[pltpu.VMEM((S + 2 * halo, C), F32), pltpu.VMEM((rows, C), F32)],
        compiler_params=_cparams(("parallel", "parallel")),
        name="dwconv_ln" if with_ln else "dwconv",
    )(x2, w, b, g, beta)


ATT_HEADS_PER_STEP = 8


def _attn_kernel(q_ref, k_ref, v_ref, o_ref):
    for pair in range(ATT_HEADS_PER_STEP // 2):
        acc = None
        for j in range(2):
            hd = 2 * pair + j
            sl = slice(hd * HEAD_PAD, (hd + 1) * HEAD_PAD)
            s = _dot_nt(q_ref[:, sl], k_ref[:, sl])
            m = jnp.max(s, -1, keepdims=True)
            p = jnp.exp2(s - m)
            l = jnp.sum(p, -1, keepdims=True)
            o = _dot(p.astype(BF16), v_ref[:, sl]) / l
            acc = o if acc is None else acc + o
        o_ref[:, pair * 2 * MLA_V:(pair + 1) * 2 * MLA_V] = acc.astype(o_ref.dtype)


def _attention(q, k, v, B, S, tq=256):
    nq = S // tq
    T = B * S
    hps = ATT_HEADS_PER_STEP
    return pl.pallas_call(
        _attn_kernel,
        grid=(B, MLA_HEADS // hps, nq),
        in_specs=[
            pl.BlockSpec((tq, hps * HEAD_PAD), lambda b, hp, i: (b * nq + i, hp)),
            pl.BlockSpec((S, hps * HEAD_PAD), lambda b, hp, i: (b, hp), pipeline_mode=pl.Buffered(1)),
            pl.BlockSpec((S, hps * HEAD_PAD), lambda b, hp, i: (b, hp), pipeline_mode=pl.Buffered(1)),
        ],
        out_specs=pl.BlockSpec((tq, hps * MLA_V), lambda b, hp, i: (b * nq + i, hp)),
        out_shape=jax.ShapeDtypeStruct((T, MLA_HEADS * MLA_V), BF16),
        compiler_params=_cparams(("parallel", "parallel", "parallel")),
        name="attention",
    )(q, k, v)


def _ev_out_ffn_kernel(x_ref, u_ref, a_ref, w1_ref, w2_ref, g1_ref, b1_ref, wg_ref, wu_ref, wd_ref, g2_ref, b2_ref,
                       o_ref, *, fc):
    m = _dot(u_ref[...], w1_ref[...]) + _dot(a_ref[...], w2_ref[...])
    x = _layernorm(ALPHA * x_ref[...] + m, g1_ref[...], b1_ref[...])
    xb = x.astype(BF16)
    acc = jnp.zeros(x.shape, F32)
    for c in range(wg_ref.shape[1] // fc):
        sl = slice(c * fc, (c + 1) * fc)
        hh = _silu(_dot(xb, wg_ref[:, sl])) * _dot(xb, wu_ref[:, sl])
        acc = acc + _dot(hh.astype(BF16), wd_ref[sl, :])
    o_ref[...] = _layernorm(ALPHA * x + acc, g2_ref[...], b2_ref[...])


def _ev_out_ffn(x2, u, att, w1, w2, g1, b1, wg, wu, wd, g2, b2, tm=512, fc=256):
    T = x2.shape[0]
    row = lambda i: (i, 0)
    consts = (w1, w2, g1, b1, wg, wu, wd, g2, b2)
    return pl.pallas_call(
        functools.partial(_ev_out_ffn_kernel, fc=fc),
        grid=(T // tm,),
        in_specs=[
            pl.BlockSpec((tm, D_MODEL), row),
            pl.BlockSpec((tm, CONV_CH), row),
            pl.BlockSpec((tm, MLA_HEADS * MLA_V), row),
        ] + [_const_spec(c.shape) for c in consts],
        out_specs=pl.BlockSpec((tm, D_MODEL), row),
        out_shape=jax.ShapeDtypeStruct((T, D_MODEL), F32),
        compiler_params=_cparams(("parallel",)),
        name="ev_out_ffn",
    )(x2, u, att, *consts)


def _od_in_kernel(x_ref, w_ref, wkt_ref, o_ref, kt_ref):
    xb = x_ref[...].astype(BF16)
    o_ref[...] = _dot(xb, w_ref[...])
    kt_ref[...] = _dot_nt(wkt_ref[...], xb) * (ML_HEAD_DIM ** -0.5)


def _od_in(x2, w, wkt, seq, tm=512):
    T = x2.shape[0]
    nps = seq // tm
    row = lambda i: (i, 0)
    return pl.pallas_call(
        _od_in_kernel,
        grid=(T // tm,),
        in_specs=[pl.BlockSpec((tm, D_MODEL), row), _const_spec(w.shape), _const_spec(wkt.shape)],
        out_specs=[pl.BlockSpec((tm, OD_COLS), row),
                   pl.BlockSpec((ML_INNER, tm), lambda i: (i // nps, i % nps))],
        out_shape=[jax.ShapeDtypeStruct((T, OD_COLS), F32),
                   jax.ShapeDtypeStruct((T // seq * ML_INNER, seq), F32)],
        compiler_params=_cparams(("parallel",)),
        name="od_in",
    )(x2, w, wkt)


def _tri(reverse):
    i = lax.broadcasted_iota(jnp.int32, (CHUNK, CHUNK), 0)
    j = lax.broadcasted_iota(jnp.int32, (CHUNK, CHUNK), 1)
    mask = (j >= i) if reverse else (j <= i)
    return mask, mask.astype(F32)


def _softplus(v):
    return jnp.maximum(v, 0.0) + jnp.log1p(jnp.exp(-jnp.abs(v)))


def _ssd_body(xbc_ref, sm_ref, bias_ref, a_ref, y_ref, st_ref, *, reverse, lane0):
    mask, tri = _tri(reverse)
    last = 0 if reverse else CHUNK - 1
    dt_all = _softplus(sm_ref[...] + bias_ref[...])
    da_all = dt_all * a_ref[...]
    cs_all = jnp.dot(tri, da_all, preferred_element_type=F32, precision=lax.Precision.HIGHEST)
    cs_t = cs_all.T
    tot_all = cs_all[last:last + 1, :]
    grow_all = jnp.exp(cs_all)
    rest_all = jnp.exp(tot_all - cs_all)
    etot_all = jnp.exp(tot_all)
    P = SSD_HEAD_DIM
    lo = lax.broadcasted_iota(jnp.int32, (CHUNK, 2 * P), 1) < P
    lo_row = lo[0:1, :]
    pairs_per_group = SSD_HEADS // SSD_GROUPS // 2
    prev = [st_ref[p] for p in range(SSD_HEADS // 2)]

    def pick(arr, l0):
        return jnp.where(lo if arr.shape[0] > 1 else lo_row, arr[:, l0:l0 + 1], arr[:, l0 + 1:l0 + 2])

    for g in range(SSD_GROUPS):
        b0 = SSD_INNER + g * SSD_STATE
        c0 = SSD_INNER + SSD_GROUPS * SSD_STATE + g * SSD_STATE
        cm = xbc_ref[:, c0:c0 + SSD_STATE].astype(BF16)
        bm_t = xbc_ref[:, b0:b0 + SSD_STATE].T.astype(BF16)
        cb = _dot(cm, bm_t)
        for pp in range(pairs_per_group):
            p = g * pairs_per_group + pp
            l0 = lane0 + 2 * p
            ps = slice(p * 2 * P, (p + 1) * 2 * P)
            xdt = xbc_ref[:, ps] * pick(dt_all, l0)
            x_lo = jnp.where(lo, xdt, 0.0).astype(BF16)
            x_hi = jnp.where(lo, 0.0, xdt).astype(BF16)
            dec0 = jnp.exp(jnp.where(mask, cs_all[:, l0:l0 + 1] - cs_t[l0:l0 + 1, :], -jnp.inf))
            dec1 = jnp.exp(jnp.where(mask, cs_all[:, l0 + 1:l0 + 2] - cs_t[l0 + 1:l0 + 2, :], -jnp.inf))
            y_diag = _dot((cb * dec0).astype(BF16), x_lo) + _dot((cb * dec1).astype(BF16), x_hi)
            y_off = _dot(cm, prev[p].astype(BF16)) * pick(grow_all, l0)
            y_ref[:, ps] = y_diag + y_off
            xw = (xdt * pick(rest_all, l0)).astype(BF16)
            st_ref[p] = pick(etot_all, l0) * prev[p] + _dot(bm_t, xw)


def _running_max(x, reverse):
    n = x.shape[0]
    row = lax.broadcasted_iota(jnp.int32, x.shape, 0)
    s = 1
    while s < n:
        if reverse:
            sh = jnp.where(row < n - s, pltpu.roll(x, n - s, 0), -jnp.inf)
        else:
            sh = jnp.where(row >= s, pltpu.roll(x, s, 0), -jnp.inf)
        x = jnp.maximum(x, sh)
        s *= 2
    return x


def _mlstm_body(q_ref, kt_ref, v_ref, ig_ref, fg_ref, ib_ref, fb_ref, h_ref, cn_ref, m_ref, *, reverse, direction):
    L = CHUNK
    mask, tri = _tri(reverse)
    last = 0 if reverse else L - 1
    li = ig_ref[...] + ib_ref[...]
    pre = fg_ref[...] + fb_ref[...]
    lf = jnp.minimum(pre, 0.0) - jnp.log1p(jnp.exp(-jnp.abs(pre)))
    bc = jnp.dot(tri, lf, preferred_element_type=F32, precision=lax.Precision.HIGHEST)
    u = li - bc
    cm = _running_max(u, reverse)
    m_prev = m_ref[...]
    big_m = jnp.maximum(m_prev, cm)
    g = bc[last:last + 1, :]
    m_loc = g + cm[last:last + 1, :]
    u_t = u.T
    e_end_t = jnp.exp(g + u - m_loc).T
    m_new = jnp.maximum(g + m_prev, m_loc)
    a_old = jnp.exp(g + m_prev - m_new)
    a_new = jnp.exp(m_loc - m_new)
    w_inter = jnp.exp(m_prev - big_m)
    emt = jnp.exp(-(bc + big_m))
    m_ref[...] = m_new

    lane = lax.broadcasted_iota(jnp.int32, (L, 128), 1)
    lo = lane < ML_HEAD_DIM
    row = lax.broadcasted_iota(jnp.int32, (128, 2 * 128), 0)
    col = lax.broadcasted_iota(jnp.int32, (128, 2 * 128), 1)
    block_diag = (row < ML_HEAD_DIM) == ((col % 128) < ML_HEAD_DIM)
    row_lo = lax.broadcasted_iota(jnp.int32, (128, L), 0) < ML_HEAD_DIM
    ones = jnp.ones((L, 128), F32)
    npairs = ML_HEADS // 2
    prev = [cn_ref[p] for p in range(npairs)]
    for p in range(npairs):
        l0 = direction * ML_HEADS + 2 * p
        l1 = l0 + 1
        ps = slice(p * 128, (p + 1) * 128)
        qp = q_ref[:, ps]
        vp = v_ref[:, ps]
        kt = kt_ref[ps, :]
        ktb = kt.astype(BF16)
        q_lo = jnp.where(lo, qp, 0.0).astype(BF16)
        q_hi = jnp.where(lo, 0.0, qp).astype(BF16)
        vo_lo = jnp.concatenate([jnp.where(lo, vp, 0.0), jnp.where(lo, ones, 0.0)], -1).astype(BF16)
        vo_hi = jnp.concatenate([jnp.where(lo, 0.0, vp), jnp.where(lo, 0.0, ones)], -1).astype(BF16)
        w0 = jnp.where(mask, jnp.exp(u_t[l0:l0 + 1, :] - big_m[:, l0:l0 + 1]), 0.0)
        w1 = jnp.where(mask, jnp.exp(u_t[l1:l1 + 1, :] - big_m[:, l1:l1 + 1]), 0.0)
        a0 = (_dot(q_lo, ktb) * w0).astype(BF16)
        a1 = (_dot(q_hi, ktb) * w1).astype(BF16)
        wi = jnp.where(lo, w_inter[:, l0:l0 + 1], w_inter[:, l1:l1 + 1])
        wi2 = jnp.concatenate([wi, wi], -1)
        nd = _dot(a0, vo_lo) + _dot(a1, vo_hi) + wi2 * _dot(qp.astype(BF16), prev[p].astype(BF16))
        floor = jnp.where(lo, emt[:, l0:l0 + 1], emt[:, l1:l1 + 1])
        h_ref[:, ps] = nd[:, :128] / jnp.maximum(jnp.abs(nd[:, 128:]), floor)
        e_t = jnp.where(row_lo, e_end_t[l0:l0 + 1, :], e_end_t[l1:l1 + 1, :])
        kte = (kt * e_t).astype(BF16)
        vo = jnp.concatenate([vp, ones], -1).astype(BF16)
        s_loc = jnp.where(block_diag, _dot(kte, vo), 0.0)
        row2 = lax.broadcasted_iota(jnp.int32, (128, 1), 0) < ML_HEAD_DIM
        ao = jnp.where(row2, a_old[:, l0:l0 + 1], a_old[:, l1:l1 + 1])
        an = jnp.where(row2, a_new[:, l0:l0 + 1], a_new[:, l1:l1 + 1])
        cn_ref[p] = ao * prev[p] + an * s_loc


def _odd_mix_kernel(xbc_f, dt_f, q_f, kt_f, v_f, ig_f, fg_f, xbc_r, dt_r, q_r, kt_r, v_r, ig_r, fg_r,
                    dtb_f, a_f, dtb_r, a_r, ib_ref, fb_ref, yf_ref, yb_ref, hf_ref, hb_ref,
                    st_f, st_r, cn_f, cn_r, m_f, m_r):
    @pl.when(pl.program_id(1) == 0)
    def _():
        for ref in (st_f, st_r, cn_f, cn_r, m_f, m_r):
            ref[...] = jnp.zeros(ref.shape, F32)

    _ssd_body(xbc_f, dt_f, dtb_f, a_f, yf_ref, st_f, reverse=False, lane0=0)
    _ssd_body(xbc_r, dt_r, dtb_r, a_r, yb_ref, st_r, reverse=True, lane0=SSD_HEADS)
    _mlstm_body(q_f, kt_f, v_f, ig_f, fg_f, ib_ref, fb_ref, hf_ref, cn_f, m_f, reverse=False, direction=0)
    _mlstm_body(q_r, kt_r, v_r, ig_r, fg_r, ib_ref, fb_ref, hb_ref, cn_r, m_r, reverse=True, direction=1)


def _odd_mix(xbc, proj, kt, S, w):
    T = proj.shape[0]
    B = T // S
    nc = S // CHUNK
    fwd = lambda c: c
    rev = lambda c: nc - 1 - c

    def specs(cidx):
        col = lambda cb: (lambda b, c: (b * nc + cidx(c), cb))
        return [
            pl.BlockSpec((CHUNK, SSD_XBC), col(0)),
            pl.BlockSpec((CHUNK, 128), col(OD_DT // 128)),
            pl.BlockSpec((CHUNK, ML_INNER), col(OD_Q // 512)),
            pl.BlockSpec((ML_INNER, CHUNK), lambda b, c: (b, cidx(c))),
            pl.BlockSpec((CHUNK, ML_INNER), col(OD_V // 512)),
            pl.BlockSpec((CHUNK, 128), col(OD_IG // 128)),
            pl.BlockSpec((CHUNK, 128), col(OD_FG // 128)),
        ]

    rows = [w["dt_bias"][0], w["a"][0], w["dt_bias"][1], w["a"][1], w["ig_b"], w["fg_b"]]
    out = lambda cidx: pl.BlockSpec((CHUNK, 512), lambda b, c: (b * nc + cidx(c), 0))
    seq = (xbc, proj, proj, kt, proj, proj, proj)
    return pl.pallas_call(
        _odd_mix_kernel,
        grid=(B, nc),
        in_specs=specs(fwd) + specs(rev) + [_const_spec(r.shape) for r in rows],
        out_specs=[out(fwd), out(rev), out(fwd), out(rev)],
        out_shape=[jax.ShapeDtypeStruct((T, 512), F32)] * 4,
        scratch_shapes=[
            pltpu.VMEM((SSD_HEADS // 2, SSD_STATE, 2 * SSD_HEAD_DIM), F32),
            pltpu.VMEM((SSD_HEADS // 2, SSD_STATE, 2 * SSD_HEAD_DIM), F32),
            pltpu.VMEM((ML_HEADS // 2, 128, 256), F32),
            pltpu.VMEM((ML_HEADS // 2, 128, 256), F32),
            pltpu.VMEM((1, 128), F32),
            pltpu.VMEM((1, 128), F32),
        ],
        compiler_params=_cparams(("parallel", "arbitrary")),
        name="odd_mix",
    )(*seq, *seq, *rows)


def _od_out_kernel(x_ref, z_ref, xs_ref, o_ref, yf_ref, yb_ref, hf_ref, hb_ref, dsk_ref, sg_ref, mg_ref, avg_ref,
                   w1_ref, w2_ref, g_ref, b_ref, rw_ref, rb_ref, x1_ref, route_ref, route_t_ref):
    y = (yf_ref[...] + yb_ref[...] + xs_ref[...] * dsk_ref[...]) * _silu(z_ref[...])
    gw = SSD_INNER // SSD_GROUPS
    m = None
    for g in range(SSD_GROUPS):
        sl = slice(g * gw, (g + 1) * gw)
        yn = _rmsnorm(y[:, sl], sg_ref[:, sl]).astype(BF16)
        t = _dot(yn, w1_ref[sl, :])
        m = t if m is None else m + t
    hs = hf_ref[...] + hb_ref[...]
    avg = avg_ref[...]

    def head_mean(v):
        hi = v.astype(BF16)
        lo = (v - hi.astype(F32)).astype(BF16)
        return _dot(hi, avg) + _dot(lo, avg)

    dv = hs - head_mean(hs)
    var = head_mean(dv * dv)
    hn = jax.nn.sigmoid(o_ref[...]) * (dv * lax.rsqrt(var + LN_EPS) * mg_ref[...])
    m = m + _dot(hn.astype(BF16), w2_ref[...])
    x1 = _layernorm(ALPHA * x_ref[...] + m, g_ref[...], b_ref[...])
    x1_ref[...] = x1
    logits = _dot(x1.astype(BF16), rw_ref[...]) + rb_ref[...]
    lane = lax.broadcasted_iota(jnp.int32, logits.shape, 1).astype(F32)
    m1 = jnp.max(logits, -1, keepdims=True)
    i1 = jnp.min(jnp.where(logits == m1, lane, 128.0), -1, keepdims=True)
    rest = jnp.where(lane == i1, -jnp.inf, logits)
    m2 = jnp.max(rest, -1, keepdims=True)
    i2 = jnp.min(jnp.where(rest == m2, lane, 128.0), -1, keepdims=True)
    e = jnp.exp(m2 - m1)
    g1 = 1.0 / (1.0 + e)
    g2 = e / (1.0 + e)
    route = jnp.where(lane == 0.0, i1,
                      jnp.where(lane == 1.0, i2, jnp.where(lane == 2.0, g1, jnp.where(lane == 3.0, g2, 0.0))))
    route_ref[...] = route
    route_t_ref[...] = route.T[0:8, :]


def _od_out(x2, proj, xbc, yf, yb, hf, hb, dsk, sg, mg, w1, w2, g, b, rw, rb, tm=256):
    T = x2.shape[0]
    row = lambda i: (i, 0)
    colb = lambda cb: (lambda i: (i, cb))
    head = jnp.arange(ML_INNER) // ML_HEAD_DIM
    avg = jnp.where(head[:, None] == head[None, :], 1.0 / ML_HEAD_DIM, 0.0).astype(BF16)
    return pl.pallas_call(
        _od_out_kernel,
        grid=(T // tm,),
        in_specs=[
            pl.BlockSpec((tm, D_MODEL), row),
            pl.BlockSpec((tm, 512), colb(OD_Z // 512)),
            pl.BlockSpec((tm, 512), colb(0)),
            pl.BlockSpec((tm, 512), colb(OD_O // 512)),
            pl.BlockSpec((tm, 512), row), pl.BlockSpec((tm, 512), row),
            pl.BlockSpec((tm, 512), row), pl.BlockSpec((tm, 512), row),
            _const_spec(dsk.shape), _const_spec(sg.shape), _const_spec(mg.shape), _const_spec(avg.shape),
            _const_spec(w1.shape), _const_spec(w2.shape), _const_spec(g.shape), _const_spec(b.shape),
            _const_spec(rw.shape), _const_spec(rb.shape),
        ],
        out_specs=[pl.BlockSpec((tm, D_MODEL), row), pl.BlockSpec((tm, 128), row),
                   pl.BlockSpec((8, tm), lambda i: (0, i))],
        out_shape=[jax.ShapeDtypeStruct((T, D_MODEL), F32), jax.ShapeDtypeStruct((T, 128), F32),
                   jax.ShapeDtypeStruct((8, T), F32)],
        compiler_params=_cparams(("parallel",)),
        name="od_out",
    )(x2, proj, xbc, proj, yf, yb, hf, hb, dsk, sg, mg, avg, w1, w2, g, b, rw, rb)


def _row_copy(src_hbm, idx, dst, r, sem):
    return pltpu.make_async_copy(src_hbm.at[pl.ds(idx, 1)], dst.at[pl.ds(r, 1)], sem)


def _gather_rows(src_hbm, idx_smem, dst, sem, n):
    for r in range(n):
        _row_copy(src_hbm, idx_smem[0, r], dst, r, sem).start(priority=r % 2)


def _gather_wait(src_hbm, dst, sem):
    pltpu.make_async_copy(src_hbm.at[pl.ds(0, dst.shape[0])], dst, sem).wait()


def _gather_loop(src_hbm, idx_smem, idx0, dst, sem):
    def start(r, carry):
        _row_copy(src_hbm, idx_smem[0, idx0 + r], dst, r, sem).start()
        return carry

    lax.fori_loop(0, dst.shape[0], start, 0, unroll=8)


def _moe_kernel(te_ref, tok0_ref, tokn_ref, x_hbm, wg_ref, wu_ref, wd_ref, o_ref,
                xbuf, xb_ref, acc_ref, sem, *, nfc, sub):
    i = pl.program_id(0)
    j = pl.program_id(1)
    n = pl.num_programs(0)
    per_step = xbuf.shape[1]

    @pl.when(jnp.logical_and(i == 0, j == 0))
    def _():
        for jj in range(nfc):
            _gather_loop(x_hbm, tok0_ref, jj * per_step, xbuf.at[jj], sem)

    @pl.when(j == 0)
    def _():
        for jj in range(nfc):
            _gather_wait(x_hbm, xbuf.at[jj], sem)
        for jj in range(nfc):
            xb_ref[jj * per_step:(jj + 1) * per_step, :] = xbuf[jj].astype(BF16)
        acc_ref[...] = jnp.zeros(acc_ref.shape, F32)

    xb = xb_ref[...]
    acc = acc_ref[...]
    for c in range(wg_ref.shape[1] // sub):
        sl = slice(c * sub, (c + 1) * sub)
        hh = _silu(_dot(xb, wg_ref[:, sl])) * _dot(xb, wu_ref[:, sl])
        acc = acc + _dot(hh.astype(BF16), wd_ref[sl, :])
    acc_ref[...] = acc
    _gather_rows(x_hbm, tokn_ref, xbuf.at[j], sem, per_step)

    @pl.when(j == nfc - 1)
    def _():
        o_ref[...] = acc_ref[...]

    @pl.when(jnp.logical_and(i == n - 1, j == nfc - 1))
    def _():
        for jj in range(nfc):
            _gather_wait(x_hbm, xbuf.at[jj], sem)


def _moe(x1, tile_expert, row_token, wg, wu, wd, tm, fc=D_FF_EXPERT, sub=256):
    n_tiles = tile_expert.shape[0]
    assert D_FF_EXPERT % fc == 0 and fc % sub == 0
    nfc = D_FF_EXPERT // fc
    per_step = tm // nfc
    wmode = pl.Buffered(1) if nfc == 1 else pl.Buffered(2)
    grid_spec = pltpu.PrefetchScalarGridSpec(
        num_scalar_prefetch=1,
        grid=(n_tiles, nfc),
        in_specs=[
            pl.BlockSpec((None, 1, tm), lambda i, j, te: (0, 0, 0), memory_space=pltpu.SMEM),
            pl.BlockSpec((None, 1, per_step), lambda i, j, te: ((i + 1) * nfc + j, 0, 0), memory_space=pltpu.SMEM),
            pl.BlockSpec(memory_space=pl.ANY),
            pl.BlockSpec((None, D_MODEL, fc), lambda i, j, te: (te[i], 0, j), pipeline_mode=wmode),
            pl.BlockSpec((None, D_MODEL, fc), lambda i, j, te: (te[i], 0, j), pipeline_mode=wmode),
            pl.BlockSpec((None, fc, D_MODEL), lambda i, j, te: (te[i], j, 0), pipeline_mode=wmode),
        ],
        out_specs=pl.BlockSpec((tm, D_MODEL), lambda i, j, te: (i, 0)),
        scratch_shapes=[
            pltpu.VMEM((nfc, per_step, D_MODEL), F32),
            pltpu.VMEM((tm, D_MODEL), BF16),
            pltpu.VMEM((tm, D_MODEL), F32),
            pltpu.SemaphoreType.DMA,
        ],
    )
    return pl.pallas_call(
        functools.partial(_moe_kernel, nfc=nfc, sub=sub),
        grid_spec=grid_spec,
        out_shape=jax.ShapeDtypeStruct((n_tiles * tm, D_MODEL), F32),
        compiler_params=_cparams(("arbitrary", "arbitrary")),
        name="moe_experts",
    )(tile_expert, row_token.reshape(n_tiles + 1, 1, tm), row_token.reshape((n_tiles + 1) * nfc, 1, per_step),
      x1, wg, wu, wd)


def _combine_kernel(pos0_ref, posa_ref, posb_ref, x_ref, route_ref, y_hbm, g_ref, b_ref, o_ref, ybuf, sem, *, tm):
    s = pl.program_id(0)
    n = pl.num_programs(0)

    @pl.when(s == 0)
    def _():
        for half in range(2):
            _gather_loop(y_hbm, pos0_ref, half * 2 * tm, ybuf.at[half], sem.at[half])

    for half, pos_ref in ((0, posa_ref), (1, posb_ref)):
        _gather_wait(y_hbm, ybuf.at[half], sem.at[half])
        rows = slice(half * tm, (half + 1) * tm)
        g0 = route_ref[rows, 2:3]
        g1 = route_ref[rows, 3:4]
        f = g0 * ybuf[half, 0:tm, :] + g1 * ybuf[half, tm:2 * tm, :]
        _gather_rows(y_hbm, pos_ref, ybuf.at[half], sem.at[half], 2 * tm)
        o_ref[rows, :] = _layernorm(ALPHA * x_ref[rows, :] + f, g_ref[...], b_ref[...])

    @pl.when(s == n - 1)
    def _():
        _gather_wait(y_hbm, ybuf.at[0], sem.at[0])
        _gather_wait(y_hbm, ybuf.at[1], sem.at[1])


def _combine(x1, route, y_sorted, pos, g, b, tm=256):
    T = route.shape[0]
    nt = T // tm
    assert nt % 2 == 0
    return pl.pallas_call(
        functools.partial(_combine_kernel, tm=tm),
        grid=(nt // 2,),
        in_specs=[
            pl.BlockSpec((None, 1, 4 * tm), lambda s: (0, 0, 0), memory_space=pltpu.SMEM),
            pl.BlockSpec((None, 1, 2 * tm), lambda s: (2 * s + 2, 0, 0), memory_space=pltpu.SMEM),
            pl.BlockSpec((None, 1, 2 * tm), lambda s: (2 * s + 3, 0, 0), memory_space=pltpu.SMEM),
            pl.BlockSpec((2 * tm, D_MODEL), lambda s: (s, 0)),
            pl.BlockSpec((2 * tm, 128), lambda s: (s, 0)),
            pl.BlockSpec(memory_space=pl.ANY),
            _const_spec(g.shape), _const_spec(b.shape),
        ],
        out_specs=pl.BlockSpec((2 * tm, D_MODEL), lambda s: (s, 0)),
        out_shape=jax.ShapeDtypeStruct((T, D_MODEL), F32),
        scratch_shapes=[pltpu.VMEM((2, 2 * tm, D_MODEL), F32), pltpu.SemaphoreType.DMA((2,))],
        compiler_params=_cparams(("arbitrary",)),
        name="moe_combine",
    )(pos.reshape((nt + 2) // 2, 1, 4 * tm), pos, pos, x1, route, y_sorted, g, b)


def _route_tables(route_t, tm_e, tm_c):
    T = route_t.shape[1]
    A = 2 * T
    e0 = route_t[0].astype(jnp.int32)
    e1 = route_t[1].astype(jnp.int32)
    ids = jnp.arange(N_EXPERTS, dtype=jnp.int32)[:, None]
    oh0 = e0[None, :] == ids
    oh1 = e1[None, :] == ids
    c0 = jnp.cumsum(oh0.astype(jnp.int32), axis=1)
    c1 = jnp.cumsum(oh1.astype(jnp.int32), axis=1)
    n0 = c0[:, -1:]
    counts = (n0 + c1[:, -1:])[:, 0]
    start = jnp.cumsum(counts) - counts
    padded = ((counts + tm_e - 1) // tm_e) * tm_e
    pend = jnp.cumsum(padded)
    pstart = pend - padded
    dest0 = jnp.sum(jnp.where(oh0, pstart[:, None] + c0 - 1, 0), axis=0)
    dest1 = jnp.sum(jnp.where(oh1, pstart[:, None] + n0 + c1 - 1, 0), axis=0)
    order = jnp.argsort(jnp.concatenate([e0, e1]), stable=True).astype(jnp.int32)
    n_tiles = A // tm_e + N_EXPERTS
    rows = jnp.arange(n_tiles * tm_e, dtype=jnp.int32)
    past = rows[None, :] >= pend[:, None]
    e_row = jnp.minimum(jnp.sum(past.astype(jnp.int32), axis=0), N_EXPERTS - 1)
    local = rows - jnp.sum(jnp.where(past, padded[:, None], 0), axis=0)
    count_row = jnp.sum(jnp.where(e_row[None, :] == ids, counts[:, None], 0), axis=0)
    ok = jnp.logical_and(local < count_row, rows < pend[-1])
    src = jnp.sum(jnp.where(past, counts[:, None], 0), axis=0) + local
    a_row = order[jnp.clip(src, 0, A - 1)]
    row_token = jnp.where(ok, jnp.where(a_row >= T, a_row - T, a_row), 0).astype(jnp.int32)
    row_token = jnp.concatenate([row_token, jnp.zeros((tm_e,), jnp.int32)])
    tile_expert = e_row[::tm_e]
    nt = T // tm_c
    pos = jnp.concatenate([dest0.reshape(nt, tm_c), dest1.reshape(nt, tm_c)], axis=1)
    pos = jnp.concatenate([pos, jnp.zeros((2, 2 * tm_c), jnp.int32)], 0).astype(jnp.int32)
    return tile_expert, row_token, pos.reshape(nt + 2, 1, 2 * tm_c)


def _rot_cols(w):
    half = MLA_ROPE // 2
    return jnp.concatenate([-w[..., half:], w[..., :half]], -1)


def _prep_even(p, j):
    w_in = p["ev_w_in"][j]
    c_rot = 2 * CONV_CH + MLA_Q_LORA + MLA_KV_LORA
    k_rot = w_in[:, c_rot:c_rot + MLA_ROPE]
    z64 = jnp.zeros((D_MODEL, MLA_NOPE), F32)
    z32 = jnp.zeros((D_MODEL, HEAD_PAD - MLA_NOPE - MLA_ROPE), F32)
    w_in2 = jnp.concatenate([w_in[:, :c_rot], z64, k_rot, z32, z64, _rot_cols(k_rot), z32], -1).astype(BF16)
    wq = p["mla_w_uq"][j].reshape(MLA_Q_LORA, MLA_HEADS, MLA_NOPE + MLA_ROPE)
    zq = jnp.zeros((MLA_Q_LORA, MLA_HEADS, HEAD_PAD - MLA_NOPE - MLA_ROPE), F32)
    zq64 = jnp.zeros((MLA_Q_LORA, MLA_HEADS, MLA_NOPE), F32)
    wq_plain = jnp.concatenate([wq, zq], -1).reshape(MLA_Q_LORA, -1)
    wq_rot = jnp.concatenate([zq64, _rot_cols(wq[..., MLA_NOPE:]), zq], -1).reshape(MLA_Q_LORA, -1)
    wq2 = jnp.concatenate([wq_plain, wq_rot], -1).astype(BF16)
    wkv = p["mla_w_ukv"][j].reshape(MLA_KV_LORA, MLA_HEADS, MLA_NOPE + MLA_V)
    zk = jnp.zeros((MLA_KV_LORA, MLA_HEADS, MLA_V), F32)
    wk = jnp.concatenate([wkv[..., :MLA_NOPE], zk], -1).reshape(MLA_KV_LORA, -1)
    wv = wkv[..., MLA_NOPE:]
    even = (jnp.arange(MLA_HEADS) % 2 == 0)[None, :, None]
    wv2 = jnp.concatenate([jnp.where(even, wv, 0.0), jnp.where(even, 0.0, wv)], -1).reshape(MLA_KV_LORA, -1)
    wkv2 = jnp.concatenate([wk, wv2], -1).astype(BF16)
    w_out = p["ev_w_out"][j].astype(BF16)
    return dict(
        w_in=w_in2, qg=p["mla_q_norm_g"][j][None], wq=wq2, kvg=p["mla_kv_norm_g"][j][None], wkv=wkv2,
        dw_w=p["conv_dw_w"][j], dw_b=p["conv_dw_b"][j][None], cln_g=p["conv_ln_g"][j][None],
        cln_b=p["conv_ln_b"][j][None], w1=w_out[:CONV_CH], w2=w_out[CONV_CH:],
        wg=p["ffn_w_gate"][j].astype(BF16), wu=p["ffn_w_up"][j].astype(BF16), wd=p["ffn_w_down"][j].astype(BF16),
    )


def _lane_row(vals, lane0):
    return jnp.zeros((1, 128), F32).at[0, lane0:lane0 + vals.shape[0]].set(vals)


def _prep_odd(p, j):
    w = p["od_w_in"][j]
    z112 = jnp.zeros((D_MODEL, 128 - 16), F32)
    w_in2 = jnp.concatenate([w[:, 0:1536], w[:, 1552:2064], w[:, 2576:3600],
                             w[:, 1536:1552], z112, w[:, 3600:3616], z112, w[:, 3616:3632], z112], -1).astype(BF16)
    wkt = w[:, 2064:2576].T.astype(BF16)
    w_out = p["od_w_out"][j].astype(BF16)
    rw = jnp.concatenate([p["moe_router_w"][j], jnp.zeros((D_MODEL, 128 - N_EXPERTS), F32)], -1).astype(BF16)
    rb = jnp.full((1, 128), -jnp.inf, F32).at[0, :N_EXPERTS].set(p["moe_router_b"][j])
    a = -jnp.exp(p["ssd_a_log"][j])
    return dict(
        w_in=w_in2, wkt=wkt, cw=p["ssd_conv_w"][j], cb=p["ssd_conv_b"][j][None],
        dt_bias=[_lane_row(p["ssd_dt_bias"][j][d], d * SSD_HEADS) for d in range(2)],
        a=[_lane_row(a[d], d * SSD_HEADS) for d in range(2)],
        ig_b=_lane_row(p["ml_igate_b"][j].reshape(-1), 0), fg_b=_lane_row(p["ml_fgate_b"][j].reshape(-1), 0),
        dsk=jnp.repeat(p["ssd_d"][j], SSD_HEAD_DIM)[None], sg=p["ssd_norm_g"][j][None], mg=p["ml_norm_g"][j][None],
        w1=w_out[:SSD_INNER], w2=w_out[SSD_INNER:], rw=rw, rb=rb,
        wg=p["moe_w_gate"][j].astype(BF16), wu=p["moe_w_up"][j].astype(BF16), wd=p["moe_w_down"][j].astype(BF16),
    )


def _rope_tables(seq):
    half = MLA_ROPE // 2
    inv_freq = ROPE_THETA ** (-jnp.arange(half, dtype=F32) / half)
    ang = jnp.arange(seq, dtype=F32)[:, None] * inv_freq
    cos2 = jnp.concatenate([jnp.cos(ang), jnp.cos(ang)], -1)
    sin2 = jnp.concatenate([jnp.sin(ang), jnp.sin(ang)], -1)
    pad = jnp.zeros((seq, HEAD_PAD - MLA_NOPE - MLA_ROPE), F32)
    cos_t = jnp.concatenate([jnp.ones((seq, MLA_NOPE), F32), cos2, pad], -1)
    sin_t = jnp.concatenate([jnp.zeros((seq, MLA_NOPE), F32), sin2, pad], -1)
    return cos_t, sin_t


def _even_layer(x2, B, S, w, ln, cos_t, sin_t):
    tm = min(512, S)
    u, q, k, v = _ev_in(x2, cos_t, sin_t, w["w_in"], w["qg"], w["wq"], w["kvg"], w["wkv"], S, tm=tm)
    uc = _dwconv(u, S, 0, w["dw_w"], w["dw_b"], w["cln_g"], w["cln_b"], with_ln=True, out_dtype=BF16)
    att = _attention(q, k, v, B, S, tq=min(256, S))
    return _ev_out_ffn(x2, uc, att, w["w1"], w["w2"], ln[0], ln[1], w["wg"], w["wu"], w["wd"], ln[2], ln[3], tm=tm)


def _odd_layer(x2, B, S, w, ln, tm_e=512, tm_c=256):
    proj, kt = _od_in(x2, w["w_in"], w["wkt"], S, tm=min(512, S))
    zeros = jnp.zeros((1, SSD_XBC), F32)
    xbc = _dwconv(proj, S, OD_XBC // 512, w["cw"], w["cb"], zeros, zeros, with_ln=False, out_dtype=F32, ncb=2)
    yf, yb, hf, hb = _odd_mix(xbc, proj, kt, S, w)
    x1, route, route_t = _od_out(x2, proj, xbc, yf, yb, hf, hb, w["dsk"], w["sg"], w["mg"],
                                 w["w1"], w["w2"], ln[0], ln[1], w["rw"], w["rb"], tm=min(512, x2.shape[0]))
    te, row_token, pos = _route_tables(route_t, tm_e, tm_c)
    y_sorted = _moe(x1, te, row_token, w["wg"], w["wu"], w["wd"], tm_e)
    return _combine(x1, route, y_sorted, pos, ln[2], ln[3], tm_c)


def _trunk(x, p):
    B, S, _ = x.shape
    x2 = x.reshape(B * S, D_MODEL)
    cos_t, sin_t = _rope_tables(S)
    for l in range(DEPTH):
        j = l // 2
        ln = (p["ln1_g"][l][None], p["ln1_b"][l][None], p["ln2_g"][l][None], p["ln2_b"][l][None])
        if l % 2 == 0:
            x2 = _even_layer(x2, B, S, _prep_even(p, j), ln, cos_t, sin_t)
        else:
            x2 = _odd_layer(x2, B, S, _prep_odd(p, j), ln)
    return x2.reshape(B, S, D_MODEL)


def kernel(x_prompt, x_sample, ev_w_in, conv_dw_w, conv_dw_b, conv_ln_g, conv_ln_b, mla_q_norm_g, mla_w_uq, mla_kv_norm_g, mla_w_ukv, ev_w_out, od_w_in, ssd_conv_w, ssd_conv_b, ssd_dt_bias, ssd_a_log, ssd_d, ssd_norm_g, ml_igate_b, ml_fgate_b, ml_norm_g, od_w_out, ffn_w_gate, ffn_w_up, ffn_w_down, moe_router_w, moe_router_b, moe_w_gate, moe_w_up, moe_w_down, ln1_g, ln1_b, ln2_g, ln2_b):
    p = dict(ev_w_in=ev_w_in, conv_dw_w=conv_dw_w, conv_dw_b=conv_dw_b, conv_ln_g=conv_ln_g, conv_ln_b=conv_ln_b,
             mla_q_norm_g=mla_q_norm_g, mla_w_uq=mla_w_uq, mla_kv_norm_g=mla_kv_norm_g, mla_w_ukv=mla_w_ukv,
             ev_w_out=ev_w_out, od_w_in=od_w_in, ssd_conv_w=ssd_conv_w, ssd_conv_b=ssd_conv_b,
             ssd_dt_bias=ssd_dt_bias, ssd_a_log=ssd_a_log, ssd_d=ssd_d, ssd_norm_g=ssd_norm_g,
             ml_igate_b=ml_igate_b, ml_fgate_b=ml_fgate_b, ml_norm_g=ml_norm_g, od_w_out=od_w_out,
             ffn_w_gate=ffn_w_gate, ffn_w_up=ffn_w_up, ffn_w_down=ffn_w_down, moe_router_w=moe_router_w,
             moe_router_b=moe_router_b, moe_w_gate=moe_w_gate, moe_w_up=moe_w_up, moe_w_down=moe_w_down,
             ln1_g=ln1_g, ln1_b=ln1_b, ln2_g=ln2_g, ln2_b=ln2_b)
    assert x_prompt.shape[1] == x_sample.shape[1]
    nb = x_prompt.shape[0]
    y = _trunk(jnp.concatenate([x_prompt, x_sample], 0), p)
    return (y[:nb], y[nb:])
```
